```python
import jax, jax.numpy as jnp
from jax import lax
import numpy as np

D_MODEL = 1024
BATCH = 2
SEQ = 8192
DEPTH = 4

D_MIX = D_MODEL
D_POOL = D_MIX // 4
POOL_WINDOWS = (2, 4, 8, 16)
N_POOL_GROUPS = len(POOL_WINDOWS)
POOL_GROUP = D_POOL // N_POOL_GROUPS
D_CONF = 3 * D_MIX // 8
D_SCONV = D_MIX - D_POOL - D_CONF
CONF_KERNEL = 31
SCONV_KERNEL = 3
D_IN = D_POOL + 2 * D_CONF + 3 * D_SCONV
N_MEM = 256
XATTN_HEADS = 4
XATTN_HEAD_DIM = D_MODEL // XATTN_HEADS
D_FF = 2816
FFN_CONV_KERNEL = 3
EPS = 1e-6

kernel_name = "hybrid_pool_conformer_shortconv_trunk"


def rms_norm(x, g):
    x32 = x.astype(jnp.float32)
    y = x32 * lax.rsqrt(jnp.mean(x32 * x32, axis=-1, keepdims=True) + EPS)
    return (y * g.astype(jnp.float32)).astype(x.dtype)


def layer_norm(x, g, b):
    x32 = x.astype(jnp.float32)
    mu = jnp.mean(x32, axis=-1, keepdims=True)
    xc = x32 - mu
    y = xc * lax.rsqrt(jnp.mean(xc * xc, axis=-1, keepdims=True) + EPS)
    return (y * g.astype(jnp.float32) + b.astype(jnp.float32)).astype(x.dtype)


def causal_dwconv(u, w):
    k, c = w.shape
    return lax.conv_general_dilated(
        u, w[:, None, :].astype(u.dtype), window_strides=(1,), padding=[(k - 1, 0)],
        dimension_numbers=("NWC", "WIO", "NWC"), feature_group_count=c)


def pool_mixer(u, maps, scale):
    t_len = u.shape[1]
    u32 = u.astype(jnp.float32)
    cs = jnp.cumsum(u32, axis=1)
    pos1 = jnp.arange(1, t_len + 1, dtype=jnp.int32)
    outs = []
    for g, w in enumerate(POOL_WINDOWS):
        sl = slice(g * POOL_GROUP, (g + 1) * POOL_GROUP)
        cs_g = cs[..., sl]
        prev = jnp.pad(cs_g, ((0, 0), (w, 0), (0, 0)))[:, :t_len]
        count = jnp.minimum(pos1, w).astype(jnp.float32)[None, :, None]
        pooled = ((cs_g - prev) / count - u32[..., sl]).astype(u.dtype)
        outs.append(jnp.einsum("btc,cd->btd", pooled, maps[g]))
    return jnp.concatenate(outs, axis=-1) * scale


def conformer_conv(a, gate, w_dw, b_dw, ln_g, ln_b):
    v = a * jax.nn.sigmoid(gate)
    v = causal_dwconv(v, w_dw) + b_dw
    v = layer_norm(v, ln_g, ln_b)
    return jax.nn.silu(v)


def short_gated_conv(bg, cg, xv, w):
    return bg * causal_dwconv(cg * xv, w)


def cross_attention(h, mem_n, wq, wk, wv, wo):
    b, t, _ = h.shape
    m = mem_n.shape[1]
    q = (h @ wq).reshape(b, t, XATTN_HEADS, XATTN_HEAD_DIM)
    k = (mem_n @ wk).reshape(b, m, XATTN_HEADS, XATTN_HEAD_DIM)
    v = (mem_n @ wv).reshape(b, m, XATTN_HEADS, XATTN_HEAD_DIM)
    s = jnp.einsum("bthd,bmhd->bhtm", q, k).astype(jnp.float32) * (XATTN_HEAD_DIM ** -0.5)
    p = jax.nn.softmax(s, axis=-1).astype(h.dtype)
    o = jnp.einsum("bhtm,bmhd->bthd", p, v).reshape(b, t, XATTN_HEADS * XATTN_HEAD_DIM)
    return o @ wo


def conv_ffn(h, w_up, w_conv, w_down):
    u = causal_dwconv(h @ w_up, w_conv)
    gate, val = jnp.split(u, 2, axis=-1)
    return (jax.nn.silu(gate) * val) @ w_down


def setup_inputs(seed: int = 0) -> dict:
    key = jax.random.key(seed)
    ks = jax.random.split(key, 26)
    f32 = jnp.float32

    def nrm(k, shape, scale):
        return jax.random.normal(k, shape, f32) * scale

    def gain(k, shape):
        return 1.0 + 0.05 * jax.random.normal(k, shape, f32)

    L, D = DEPTH, D_MODEL
    return {
        "x": nrm(ks[0], (BATCH, SEQ, D), 1.0),
        "mem": nrm(ks[1], (BATCH, N_MEM, D), 1.0),
        "mem_norm": gain(ks[2], (D,)),
        "mix_pre_norm": gain(ks[3], (L, D)),
        "mix_post_norm": gain(ks[4], (L, D)),
        "w_in": nrm(ks[5], (L, D, D_IN), D ** -0.5),
        "pool_maps": nrm(ks[6], (L, N_POOL_GROUPS, POOL_GROUP, POOL_GROUP), POOL_GROUP ** -0.5),
        "pool_scale": gain(ks[7], (L, D_POOL)),
        "conf_dw_w": nrm(ks[8], (L, CONF_KERNEL, D_CONF), CONF_KERNEL ** -0.5),
        "conf_dw_b": nrm(ks[9], (L, D_CONF), 0.01),
        "conf_ln_g": gain(ks[10], (L, D_CONF)),
        "conf_ln_b": nrm(ks[11], (L, D_CONF), 0.01),
        "sconv_w": nrm(ks[12], (L, SCONV_KERNEL, D_SCONV), SCONV_KERNEL ** -0.5),
        "w_out": nrm(ks[13], (L, D_MIX, D), D_MIX ** -0.5),
        "xattn_pre_norm": gain(ks[14], (L, D)),
        "xattn_post_norm": gain(ks[15], (L, D)),
        "xattn_wq": nrm(ks[16], (L, D, D), D ** -0.5),
        "xattn_wk": nrm(ks[17], (L, D, D), D ** -0.5),
        "xattn_wv": nrm(ks[18], (L, D, D), D ** -0.5),
        "xattn_wo": nrm(ks[19], (L, D, D), D ** -0.5),
        "ffn_pre_norm": gain(ks[20], (L, D)),
        "ffn_post_norm": gain(ks[21], (L, D)),
        "ffn_w_up": nrm(ks[22], (L, D, 2 * D_FF), D ** -0.5),
        "ffn_conv_w": nrm(ks[23], (L, FFN_CONV_KERNEL, 2 * D_FF), FFN_CONV_KERNEL ** -0.5),
        "ffn_w_down": nrm(ks[24], (L, D_FF, D), D_FF ** -0.5),
    }


def reference(x, mem, mem_norm, mix_pre_norm, mix_post_norm, w_in, pool_maps, pool_scale,
              conf_dw_w, conf_dw_b, conf_ln_g, conf_ln_b, sconv_w, w_out,
              xattn_pre_norm, xattn_post_norm, xattn_wq, xattn_wk, xattn_wv, xattn_wo,
              ffn_pre_norm, ffn_post_norm, ffn_w_up, ffn_conv_w, ffn_w_down):
    split_at = [D_POOL, D_POOL + D_CONF, D_POOL + 2 * D_CONF,
                D_POOL + 2 * D_CONF + D_SCONV, D_POOL + 2 * D_CONF + 2 * D_SCONV]
    mem_n = rms_norm(mem, mem_norm)
    for l in range(DEPTH):
        h = rms_norm(x, mix_pre_norm[l])
        z = h @ w_in[l]
        zp, za, zg, zb, zc, zx = jnp.split(z, split_at, axis=-1)
        ya = pool_mixer(zp, pool_maps[l], pool_scale[l])
        yb = conformer_conv(za, zg, conf_dw_w[l], conf_dw_b[l], conf_ln_g[l], conf_ln_b[l])
        yc = short_gated_conv(zb, zc, zx, sconv_w[l])
        y = jnp.concatenate([ya, yb, yc], axis=-1) @ w_out[l]
        x = x + rms_norm(y, mix_post_norm[l])
        h = rms_norm(x, xattn_pre_norm[l])
        y = cross_attention(h, mem_n, xattn_wq[l], xattn_wk[l], xattn_wv[l], xattn_wo[l])
        x = x + rms_norm(y, xattn_post_norm[l])
        h = rms_norm(x, ffn_pre_norm[l])
        y = conv_ffn(h, ffn_w_up[l], ffn_conv_w[l], ffn_w_down[l])
        x = x + rms_norm(y, ffn_post_norm[l])
    return x
```

```python
import functools

import jax
import jax.numpy as jnp
from jax import lax
from jax.experimental import pallas as pl
from jax.experimental.pallas import tpu as pltpu

F32 = jnp.float32
BF16 = jnp.bfloat16

EPS = 1e-6
LANES = 128
POOL_WINDOWS = (2, 4, 8, 16)
CONF_KERNEL = 31
SCONV_KERNEL = 3
FFN_CONV_KERNEL = 3
XATTN_HEADS = 4

TS = 512
HALO = 32
FFN_HALO = 8
RC = 64
FF_CHUNK = 256
VMEM_LIMIT = 56 * 1024 * 1024


def _rms_scale(x, g):
    ms = jnp.mean(x * x, axis=-1, keepdims=True)
    return x * lax.rsqrt(ms + EPS) * g


def _sigmoid(x):
    return 1.0 / (1.0 + jnp.exp(-x))


def _resident(shape, index_map):
    return pl.BlockSpec(shape, index_map, pipeline_mode=pl.Buffered(1))


def _kv_kernel(mem_ref, g_ref, wk_ref, wv_ref, k_ref, v_ref):
    mn = _rms_scale(mem_ref[...], g_ref[...]).astype(BF16)
    k_ref[...] = jnp.dot(mn, wk_ref[...], preferred_element_type=F32).astype(BF16)
    v_ref[...] = jnp.dot(mn, wv_ref[...], preferred_element_type=F32).astype(BF16)


def _kv_call(mem2d, mem_norm, wk, wv):
    L, D, _ = wk.shape
    R = mem2d.shape[0]
    return pl.pallas_call(
        _kv_kernel,
        grid=(L,),
        in_specs=[
            pl.BlockSpec((R, D), lambda l: (0, 0)),
            pl.BlockSpec((1, D), lambda l: (0, 0)),
            pl.BlockSpec((None, D, D), lambda l: (l, 0, 0)),
            pl.BlockSpec((None, D, D), lambda l: (l, 0, 0)),
        ],
        out_specs=[
            pl.BlockSpec((None, R, D), lambda l: (l, 0, 0)),
            pl.BlockSpec((None, R, D), lambda l: (l, 0, 0)),
        ],
        out_shape=[jax.ShapeDtypeStruct((L, R, D), BF16)] * 2,
        compiler_params=pltpu.CompilerParams(
            dimension_semantics=("arbitrary",), vmem_limit_bytes=VMEM_LIMIT),
        name="kv_proj",
    )(mem2d, mem_norm, wk, wv)


def _mixer_kernel(x_ref, g1_ref, win_ref, pmap_ref, psc_ref, cw_ref, cb_ref, lg_ref, lb_ref,
                  sw_ref, wout_ref, g2_ref, o_ref, p_scr, v_scr, c_scr, b_scr, y_scr,
                  *, d_pool, d_conf, d_sconv):
    j = pl.program_id(1)
    n_p, n_c, n_s = d_pool // LANES, d_conf // LANES, d_sconv // LANES
    o_a, o_g = d_pool, d_pool + d_conf
    o_b = d_pool + 2 * d_conf
    o_c, o_x = o_b + d_sconv, o_b + 2 * d_sconv

    @pl.when(j == 0)
    def _zero_halo():
        p_scr[:, 0:HALO, :] = jnp.zeros((n_p, HALO, LANES), F32)
        v_scr[:, 0:HALO, :] = jnp.zeros((n_c, HALO, LANES), F32)
        c_scr[:, 0:HALO, :] = jnp.zeros((n_s, HALO, LANES), F32)

    h = _rms_scale(x_ref[...], g1_ref[...]).astype(BF16)
    z = jnp.dot(h, win_ref[...], preferred_element_type=F32)

    def col(off, i):
        return z[:, off + i * LANES: off + (i + 1) * LANES]

    for i in range(n_p):
        p_scr[i, HALO:HALO + TS, :] = col(0, i)
    for i in range(n_c):
        v_scr[i, HALO:HALO + TS, :] = col(o_a, i) * _sigmoid(col(o_g, i))
    for i in range(n_s):
        b_scr[i] = col(o_b, i)
        c_scr[i, HALO:HALO + TS, :] = col(o_c, i) * col(o_x, i)

    t0 = j * TS

    def body(r, carry):
        r0 = pl.multiple_of(r * RC, RC)
        rows = pl.ds(r0, RC)

        def shifted(scr, i, k):
            return scr[i, pl.ds(r0 + (HALO - k), RC), :]

        lane = lax.broadcasted_iota(jnp.int32, (RC, LANES), 1)
        row = lax.broadcasted_iota(jnp.int32, (RC, LANES), 0)
        pos1 = t0 + r0 + row + 1
        low = lane < (LANES // 2)
        for i in range(n_p):
            w_small, w_big = POOL_WINDOWS[2 * i], POOL_WINDOWS[2 * i + 1]
            u0 = shifted(p_scr, i, 0)
            s_small = u0
            for k in range(1, w_small):
                s_small = s_small + shifted(p_scr, i, k)
            s_big = s_small
            for k in range(w_small, w_big):
                s_big = s_big + shifted(p_scr, i, k)
            cnt = jnp.minimum(pos1, jnp.where(low, w_small, w_big)).astype(F32)
            pooled = jnp.where(low, s_small, s_big) / cnt - u0
            y_scr[rows, i * LANES:(i + 1) * LANES] = pooled.astype(BF16)

        conv = []
        for i in range(n_c):
            cs = slice(i * LANES, (i + 1) * LANES)
            acc = cb_ref[:, cs] + cw_ref[CONF_KERNEL - 1:CONF_KERNEL, cs] * shifted(v_scr, i, 0)
            for k in range(1, CONF_KERNEL):
                acc = acc + cw_ref[CONF_KERNEL - 1 - k:CONF_KERNEL - k, cs] * shifted(v_scr, i, k)
            conv.append(acc)
        tot = conv[0]
        for i in range(1, n_c):
            tot = tot + conv[i]
        mu = jnp.sum(tot, axis=-1, keepdims=True) * (1.0 / d_conf)
        cen = [c - mu for c in conv]
        sq = cen[0] * cen[0]
        for i in range(1, n_c):
            sq = sq + cen[i] * cen[i]
        rstd = lax.rsqrt(jnp.sum(sq, axis=-1, keepdims=True) * (1.0 / d_conf) + EPS)
        for i in range(n_c):
            cs = slice(i * LANES, (i + 1) * LANES)
            yv = cen[i] * rstd * lg_ref[:, cs] + lb_ref[:, cs]
            y_scr[rows, d_pool + i * LANES: d_pool + (i + 1) * LANES] = (yv * _sigmoid(yv)).astype(BF16)

        for i in range(n_s):
            cs = slice(i * LANES, (i + 1) * LANES)
            acc = sw_ref[SCONV_KERNEL - 1:SCONV_KERNEL, cs] * shifted(c_scr, i, 0)
            for k in range(1, SCONV_KERNEL):
                acc = acc + sw_ref[SCONV_KERNEL - 1 - k:SCONV_KERNEL - k, cs] * shifted(c_scr, i, k)
            off = d_pool + d_conf + i * LANES
            y_scr[rows, off:off + LANES] = (b_scr[i, rows, :] * acc).astype(BF16)
        return carry

    lax.fori_loop(0, TS // RC, body, 0)

    p_scr[:, 0:HALO, :] = p_scr[:, TS:TS + HALO, :]
    v_scr[:, 0:HALO, :] = v_scr[:, TS:TS + HALO, :]
    c_scr[:, 0:HALO, :] = c_scr[:, TS:TS + HALO, :]

    ya = jnp.dot(y_scr[:, 0:d_pool], pmap_ref[...], preferred_element_type=F32) * psc_ref[...]
    y_scr[:, 0:d_pool] = ya.astype(BF16)
    y = jnp.dot(y_scr[...], wout_ref[...], preferred_element_type=F32)
    o_ref[...] = x_ref[...] + _rms_scale(y, g2_ref[...])


def _mixer_call(l, x, g1, w_in, pmap, psc, cw, cb, lg, lb, sw, w_out, g2):
    B, T, D = x.shape
    d_in = w_in.shape[-1]
    d_pool = pmap.shape[-1]
    d_conf = cw.shape[-1]
    d_sconv = sw.shape[-1]
    d_mix = w_out.shape[1]
    lay = lambda b, j: (l, 0, 0)
    vec = lambda c: pl.BlockSpec((None, 1, c), lay)
    kern = functools.partial(_mixer_kernel, d_pool=d_pool, d_conf=d_conf, d_sconv=d_sconv)
    return pl.pallas_call(
        kern,
        grid=(B, T // TS),
        in_specs=[
            pl.BlockSpec((None, TS, D), lambda b, j: (b, j, 0)),
            vec(D),
            _resident((None, D, d_in), lay),
            _resident((None, d_pool, d_pool), lay),
            vec(d_pool),
            pl.BlockSpec((None, CONF_KERNEL, d_conf), lay),
            vec(d_conf), vec(d_conf), vec(d_conf),
            pl.BlockSpec((None, SCONV_KERNEL, d_sconv), lay),
            _resident((None, d_mix, D), lay),
            vec(D),
        ],
        out_specs=pl.BlockSpec((None, TS, D), lambda b, j: (b, j, 0)),
        out_shape=jax.ShapeDtypeStruct((B, T, D), F32),
        scratch_shapes=[
            pltpu.VMEM((d_pool // LANES, HALO + TS, LANES), F32),
            pltpu.VMEM((d_conf // LANES, HALO + TS, LANES), F32),
            pltpu.VMEM((d_sconv // LANES, HALO + TS, LANES), F32),
            pltpu.VMEM((d_sconv // LANES, TS, LANES), F32),
            pltpu.VMEM((TS, d_mix), BF16),
        ],
        compiler_params=pltpu.CompilerParams(
            dimension_semantics=("arbitrary", "arbitrary"), vmem_limit_bytes=VMEM_LIMIT),
        name="mixer",
    )(x, g1, w_in, pmap, psc, cw, cb, lg, lb, sw, w_out, g2)


def _xattn_kernel(x_ref, g1_ref, wq_ref, k_ref, v_ref, wo_ref, g2_ref, o_ref, o_scr, *, head_dim):
    x = x_ref[...]
    h = _rms_scale(x, g1_ref[...]).astype(BF16)
    q = (jnp.dot(h, wq_ref[...], preferred_element_type=F32) * (head_dim ** -0.5)).astype(BF16)
    for hd in range(XATTN_HEADS):
        sl = slice(hd * head_dim, (hd + 1) * head_dim)
        s = lax.dot_general(q[:, sl], k_ref[:, sl], (((1,), (1,)), ((), ())),
                            preferred_element_type=F32)
        m = jnp.max(s, axis=-1, keepdims=True)
        p = jnp.exp(s - m)
        inv_l = 1.0 / jnp.sum(p, axis=-1, keepdims=True)
        o = jnp.dot(p.astype(BF16), v_ref[:, sl], preferred_element_type=F32) * inv_l
        o_scr[:, sl] = o.astype(BF16)
    y = jnp.dot(o_scr[...], wo_ref[...], preferred_element_type=F32)
    o_ref[...] = x + _rms_scale(y, g2_ref[...])


def _xattn_call(l, x, g1, wq, k_all, v_all, wo, g2, n_mem):
    B, T, D = x.shape
    lay = lambda b, j: (l, 0, 0)
    vec = pl.BlockSpec((None, 1, D), lay)
    kern = functools.partial(_xattn_kernel, head_dim=D // XATTN_HEADS)
    return pl.pallas_call(
        kern,
        grid=(B, T // TS),
        in_specs=[
            pl.BlockSpec((None, TS, D), lambda b, j: (b, j, 0)),
            vec,
            _resident((None, D, D), lay),
            pl.BlockSpec((None, n_mem, D), lambda b, j: (l, b, 0)),
            pl.BlockSpec((None, n_mem, D), lambda b, j: (l, b, 0)),
            _resident((None, D, D), lay),
            vec,
        ],
        out_specs=pl.BlockSpec((None, TS, D), lambda b, j: (b, j, 0)),
        out_shape=jax.ShapeDtypeStruct((B, T, D), F32),
        scratch_shapes=[pltpu.VMEM((TS, D), BF16)],
        compiler_params=pltpu.CompilerParams(
            dimension_semantics=("arbitrary", "arbitrary"), vmem_limit_bytes=VMEM_LIMIT),
        name="xattn",
    )(x, g1, wq, k_all, v_all, wo, g2)


def _ffn_kernel(x_ref, g1_ref, wup_ref, cw_ref, wdn_ref, g2_ref, o_ref,
                h_scr, u_scr, hal_scr, a_scr, *, n_chunks):
    j = pl.program_id(1)
    tiles = FF_CHUNK // LANES

    @pl.when(j == 0)
    def _zero_halo():
        hal_scr[...] = jnp.zeros(hal_scr.shape, F32)

    h_scr[...] = _rms_scale(x_ref[...], g1_ref[...]).astype(BF16)

    def chunk(c, carry):
        for half in range(2):
            wi = c + half * n_chunks
            u = jnp.dot(h_scr[...], wup_ref[wi], preferred_element_type=F32)
            for i in range(tiles):
                ct = wi * tiles + i
                s = half * tiles + i
                u_scr[s, 0:FFN_HALO, :] = hal_scr[ct]
                u_scr[s, FFN_HALO:FFN_HALO + TS, :] = u[:, i * LANES:(i + 1) * LANES]
                hal_scr[ct] = u[TS - FFN_HALO:TS, i * LANES:(i + 1) * LANES]

        def rows_body(r, carry2):
            r0 = pl.multiple_of(r * RC, RC)

            def conv(s, ct):
                w = cw_ref[ct]
                acc = w[FFN_CONV_KERNEL - 1:FFN_CONV_KERNEL, :] * u_scr[s, pl.ds(r0 + FFN_HALO, RC), :]
                for k in range(1, FFN_CONV_KERNEL):
                    acc = acc + (w[FFN_CONV_KERNEL - 1 - k:FFN_CONV_KERNEL - k, :]
                                 * u_scr[s, pl.ds(r0 + (FFN_HALO - k), RC), :])
                return acc

            for i in range(tiles):
                gate = conv(i, c * tiles + i)
                val = conv(tiles + i, (c + n_chunks) * tiles + i)
                a = gate * _sigmoid(gate) * val
                a_scr[c, pl.ds(r0, RC), i * LANES:(i + 1) * LANES] = a.astype(BF16)
            return carry2

        lax.fori_loop(0, TS // RC, rows_body, 0)
        return carry

    lax.fori_loop(0, n_chunks, chunk, 0)

    def down(c, acc):
        return acc + jnp.dot(a_scr[c], wdn_ref[c], preferred_element_type=F32)

    y = lax.fori_loop(0, n_chunks, down, jnp.zeros((TS, o_ref.shape[-1]), F32))
    o_ref[...] = x_ref[...] + _rms_scale(y, g2_ref[...])


def _ffn_call(l, x, g1, w_up, cw, w_dn, g2):
    B, T, D = x.shape
    n_up = w_up.shape[1]
    n_chunks = n_up // 2
    n_ct = cw.shape[1]
    lay3 = lambda b, j: (l, 0, 0)
    lay4 = lambda b, j: (l, 0, 0, 0)
    vec = pl.BlockSpec((None, 1, D), lay3)
    kern = functools.partial(_ffn_kernel, n_chunks=n_chunks)
    return pl.pallas_call(
        kern,
        grid=(B, T // TS),
        in_specs=[
            pl.BlockSpec((None, TS, D), lambda b, j: (b, j, 0)),
            vec,
            _resident((None, n_up, D, FF_CHUNK), lay4),
            pl.BlockSpec((None, n_ct, FFN_CONV_KERNEL, LANES), lay4),
            _resident((None, n_chunks, FF_CHUNK, D), lay4),
            vec,
        ],
        out_specs=pl.BlockSpec((None, TS, D), lambda b, j: (b, j, 0)),
        out_shape=jax.ShapeDtypeStruct((B, T, D), F32),
        scratch_shapes=[
            pltpu.VMEM((TS, D), BF16),
            pltpu.VMEM((2 * FF_CHUNK // LANES, FFN_HALO + TS, LANES), F32),
            pltpu.VMEM((n_ct, FFN_HALO, LANES), F32),
            pltpu.VMEM((n_chunks, TS, FF_CHUNK), BF16),
        ],
        compiler_params=pltpu.CompilerParams(
            dimension_semantics=("arbitrary", "arbitrary"), vmem_limit_bytes=VMEM_LIMIT),
        name="ffn",
    )(x, g1, w_up, cw, w_dn, g2)


def _block_diag(maps):
    L, G, P, _ = maps.shape
    eye = jnp.eye(G, dtype=maps.dtype)
    return jnp.einsum("lgpq,gh->lgphq", maps, eye).reshape(L, G * P, G * P)


def kernel(x, mem, mem_norm, mix_pre_norm, mix_post_norm, w_in, pool_maps, pool_scale, conf_dw_w, conf_dw_b, conf_ln_g, conf_ln_b, sconv_w, w_out, xattn_pre_norm, xattn_post_norm, xattn_wq, xattn_wk, xattn_wv, xattn_wo, ffn_pre_norm, ffn_post_norm, ffn_w_up, ffn_conv_w, ffn_w_down):
    B, T, D = x.shape
    L = w_in.shape[0]
    n_mem = mem.shape[1]
    d_ff = ffn_w_down.shape[1]
    assert T % TS == 0 and d_ff % FF_CHUNK == 0 and D % LANES == 0

    row = lambda p: p.reshape(L, 1, p.shape[-1])
    bf = lambda w: w.astype(BF16)

    n_chunks = d_ff // FF_CHUNK
    w_up_c = bf(ffn_w_up).reshape(L, D, 2 * n_chunks, FF_CHUNK).transpose(0, 2, 1, 3)
    w_dn_c = bf(ffn_w_down).reshape(L, n_chunks, FF_CHUNK, D)
    ffn_cw = ffn_conv_w.reshape(L, FFN_CONV_KERNEL, 2 * d_ff // LANES, LANES).transpose(0, 2, 1, 3)
    pmap = bf(_block_diag(pool_maps))

    k_all, v_all = _kv_call(mem.reshape(B * n_mem, D), mem_norm.reshape(1, D), bf(xattn_wk), bf(xattn_wv))
    w_in_b, w_out_b, wq_b, wo_b = bf(w_in), bf(w_out), bf(xattn_wq), bf(xattn_wo)

    for l in range(L):
        x = _mixer_call(l, x, row(mix_pre_norm), w_in_b, pmap, row(pool_scale), conf_dw_w,
                        row(conf_dw_b), row(conf_ln_g), row(conf_ln_b), sconv_w, w_out_b,
                        row(mix_post_norm))
        x = _xattn_call(l, x, row(xattn_pre_norm), wq_b, k_all, v_all, wo_b, row(xattn_post_norm), n_mem)
        x = _ffn_call(l, x, row(ffn_pre_norm), w_up_c, ffn_cw, w_dn_c, row(ffn_post_norm))
    return x
```

```python
import functools

import jax
import jax.numpy as jnp
from jax import lax
from jax.experimental import pallas as pl
from jax.experimental.pallas import tpu as pltpu

F32 = jnp.float32
BF16 = jnp.bfloat16

EPS = 1e-6
LANES = 128
POOL_WINDOWS = (2, 4, 8, 16)
CONF_KERNEL = 31
SCONV_KERNEL = 3
FFN_CONV_KERNEL = 3
XATTN_HEADS = 4

TS = 512
HALO = 32
FFN_HALO = 8
RC = 64
FF_CHUNK = 256
VMEM_LIMIT = 56 * 1024 * 1024


def _rms_scale(x, g):
    ms = jnp.mean(x * x, axis=-1, keepdims=True)
    return x * lax.rsqrt(ms + EPS) * g


def _sigmoid(x):
    return 1.0 / (1.0 + jnp.exp(-x))


def _resident(shape, index_map):
    return pl.BlockSpec(shape, index_map, pipeline_mode=pl.Buffered(1))


def _kv_kernel(mem_ref, g_ref, wk_ref, wv_ref, k_ref, v_ref):
    mn = _rms_scale(mem_ref[...], g_ref[...]).astype(BF16)
    k_ref[...] = jnp.dot(mn, wk_ref[...], preferred_element_type=F32).astype(BF16)
    v_ref[...] = jnp.dot(mn, wv_ref[...], preferred_element_type=F32).astype(BF16)


def _kv_call(mem2d, mem_norm, wk, wv):
    L, D, _ = wk.shape
    R = mem2d.shape[0]
    return pl.pallas_call(
        _kv_kernel,
        grid=(L,),
        in_specs=[
            pl.BlockSpec((R, D), lambda l: (0, 0)),
            pl.BlockSpec((1, D), lambda l: (0, 0)),
            pl.BlockSpec((None, D, D), lambda l: (l, 0, 0)),
            pl.BlockSpec((None, D, D), lambda l: (l, 0, 0)),
        ],
        out_specs=[
            pl.BlockSpec((None, R, D), lambda l: (l, 0, 0)),
            pl.BlockSpec((None, R, D), lambda l: (l, 0, 0)),
        ],
        out_shape=[jax.ShapeDtypeStruct((L, R, D), BF16)] * 2,
        compiler_params=pltpu.CompilerParams(
            dimension_semantics=("arbitrary",), vmem_limit_bytes=VMEM_LIMIT),
        name="kv_proj",
    )(mem2d, mem_norm, wk, wv)


def _mixer_kernel(x_ref, g1_ref, win_ref, pmap_ref, psc_ref, cw_ref, cb_ref, lg_ref, lb_ref,
                  sw_ref, wout_ref, g2_ref, o_ref, p_scr, v_scr, c_scr, b_scr, y_scr,
                  *, d_pool, d_conf, d_sconv):
    j = pl.program_id(1)
    n_p, n_c, n_s = d_pool // LANES, d_conf // LANES, d_sconv // LANES
    o_a, o_g = d_pool, d_pool + d_conf
    o_b = d_pool + 2 * d_conf
    o_c, o_x = o_b + d_sconv, o_b + 2 * d_sconv

    @pl.when(j == 0)
    def _zero_halo():
        p_scr[:, 0:HALO, :] = jnp.zeros((n_p, HALO, LANES), F32)
        v_scr[:, 0:HALO, :] = jnp.zeros((n_c, HALO, LANES), F32)
        c_scr[:, 0:HALO, :] = jnp.zeros((n_s, HALO, LANES), F32)

    h = _rms_scale(x_ref[...], g1_ref[...]).astype(BF16)
    z = jnp.dot(h, win_ref[...], preferred_element_type=F32)

    def col(off, i):
        return z[:, off + i * LANES: off + (i + 1) * LANES]

    for i in range(n_p):
        p_scr[i, HALO:HALO + TS, :] = col(0, i)
    for i in range(n_c):
        v_scr[i, HALO:HALO + TS, :] = col(o_a, i) * _sigmoid(col(o_g, i))
    for i in range(n_s):
        b_scr[i] = col(o_b, i)
        c_scr[i, HALO:HALO + TS, :] = col(o_c, i) * col(o_x, i)

    t0 = j * TS

    def body(r, carry):
        r0 = pl.multiple_of(r * RC, RC)
        rows = pl.ds(r0, RC)

        def shifted(scr, i, k):
            return scr[i, pl.ds(r0 + (HALO - k), RC), :]

        lane = lax.broadcasted_iota(jnp.int32, (RC, LANES), 1)
        row = lax.broadcasted_iota(jnp.int32, (RC, LANES), 0)
        pos1 = t0 + r0 + row + 1
        low = lane < (LANES // 2)
        for i in range(n_p):
            w_small, w_big = POOL_WINDOWS[2 * i], POOL_WINDOWS[2 * i + 1]
            u0 = shifted(p_scr, i, 0)
            s_small = u0
            for k in range(1, w_small):
                s_small = s_small + shifted(p_scr, i, k)
            s_big = s_small
            for k in range(w_small, w_big):
                s_big = s_big + shifted(p_scr, i, k)
            cnt = jnp.minimum(pos1, jnp.where(low, w_small, w_big)).astype(F32)
            pooled = jnp.where(low, s_small, s_big) / cnt - u0
            y_scr[rows, i * LANES:(i + 1) * LANES] = pooled.astype(BF16)

        conv = []
        for i in range(n_c):
            cs = slice(i * LANES, (i + 1) * LANES)
            acc = cb_ref[:, cs] + cw_ref[CONF_KERNEL - 1:CONF_KERNEL, cs] * shifted(v_scr, i, 0)
            for k in range(1, CONF_KERNEL):
                acc = acc + cw_ref[CONF_KERNEL - 1 - k:CONF_KERNEL - k, cs] * shifted(v_scr, i, k)
            conv.append(acc)
        tot = conv[0]
        for i in range(1, n_c):
            tot = tot + conv[i]
        mu = jnp.sum(tot, axis=-1, keepdims=True) * (1.0 / d_conf)
        cen = [c - mu for c in conv]
        sq = cen[0] * cen[0]
        for i in range(1, n_c):
            sq = sq + cen[i] * cen[i]
        rstd = lax.rsqrt(jnp.sum(sq, axis=-1, keepdims=True) * (1.0 / d_conf) + EPS)
        for i in range(n_c):
            cs = slice(i * LANES, (i + 1) * LANES)
            yv = cen[i] * rstd * lg_ref[:, cs] + lb_ref[:, cs]
            y_scr[rows, d_pool + i * LANES: d_pool + (i + 1) * LANES] = (yv * _sigmoid(yv)).astype(BF16)

        for i in range(n_s):
            cs = slice(i * LANES, (i + 1) * LANES)
            acc = sw_ref[SCONV_KERNEL - 1:SCONV_KERNEL, cs] * shifted(c_scr, i, 0)
            for k in range(1, SCONV_KERNEL):
                acc = acc + sw_ref[SCONV_KERNEL - 1 - k:SCONV_KERNEL - k, cs] * shifted(c_scr, i, k)
            off = d_pool + d_conf + i * LANES
            y_scr[rows, off:off + LANES] = (b_scr[i, rows, :] * acc).astype(BF16)
        return carry

    lax.fori_loop(0, TS // RC, body, 0)

    p_scr[:, 0:HALO, :] = p_scr[:, TS:TS + HALO, :]
    v_scr[:, 0:HALO, :] = v_scr[:, TS:TS + HALO, :]
    c_scr[:, 0:HALO, :] = c_scr[:, TS:TS + HALO, :]

    ya = jnp.dot(y_scr[:, 0:d_pool], pmap_ref[...], preferred_element_type=F32) * psc_ref[...]
    y_scr[:, 0:d_pool] = ya.astype(BF16)
    y = jnp.dot(y_scr[...], wout_ref[...], preferred_element_type=F32)
    o_ref[...] = x_ref[...] + _rms_scale(y, g2_ref[...])


def _mixer_call(l, x, g1, w_in, pmap, psc, cw, cb, lg, lb, sw, w_out, g2):
    B, T, D = x.shape
    d_in = w_in.shape[-1]
    d_pool = pmap.shape[-1]
    d_conf = cw.shape[-1]
    d_sconv = sw.shape[-1]
    d_mix = w_out.shape[1]
    lay = lambda b, j: (l, 0, 0)
    vec = lambda c: pl.BlockSpec((None, 1, c), lay)
    kern = functools.partial(_mixer_kernel, d_pool=d_pool, d_conf=d_conf, d_sconv=d_sconv)
    return pl.pallas_call(
        kern,
        grid=(B, T // TS),
        in_specs=[
            pl.BlockSpec((None, TS, D), lambda b, j: (b, j, 0)),
            vec(D),
            _resident((None, D, d_in), lay),
            _resident((None, d_pool, d_pool), lay),
            vec(d_pool),
            pl.BlockSpec((None, CONF_KERNEL, d_conf), lay),
            vec(d_conf), vec(d_conf), vec(d_conf),
            pl.BlockSpec((None, SCONV_KERNEL, d_sconv), lay),
            _resident((None, d_mix, D), lay),
            vec(D),
        ],
        out_specs=pl.BlockSpec((None, TS, D), lambda b, j: (b, j, 0)),
        out_shape=jax.ShapeDtypeStruct((B, T, D), F32),
        scratch_shapes=[
            pltpu.VMEM((d_pool // LANES, HALO + TS, LANES), F32),
            pltpu.VMEM((d_conf // LANES, HALO + TS, LANES), F32),
            pltpu.VMEM((d_sconv // LANES, HALO + TS, LANES), F32),
            pltpu.VMEM((d_sconv // LANES, TS, LANES), F32),
            pltpu.VMEM((TS, d_mix), BF16),
        ],
        compiler_params=pltpu.CompilerParams(
            dimension_semantics=("arbitrary", "arbitrary"), vmem_limit_bytes=VMEM_LIMIT),
        name="mixer",
    )(x, g1, w_in, pmap, psc, cw, cb, lg, lb, sw, w_out, g2)


def _xattn_kernel(x_ref, g1_ref, wq_ref, k_ref, v_ref, wo_ref, g2_ref, o_ref, o_scr, *, head_dim):
    x = x_ref[...]
    h = _rms_scale(x, g1_ref[...]).astype(BF16)
    q = (jnp.dot(h, wq_ref[...], preferred_element_type=F32) * (head_dim ** -0.5)).astype(BF16)
    for hd in range(XATTN_HEADS):
        sl = slice(hd * head_dim, (hd + 1) * head_dim)
        s = lax.dot_general(q[:, sl], k_ref[:, sl], (((1,), (1,)), ((), ())),
                            preferred_element_type=F32)
        m = jnp.max(s, axis=-1, keepdims=True)
        p = jnp.exp(s - m)
        inv_l = 1.0 / jnp.sum(p, axis=-1, keepdims=True)
        o = jnp.dot(p.astype(BF16), v_ref[:, sl], preferred_element_type=F32) * inv_l
        o_scr[:, sl] = o.astype(BF16)
    y = jnp.dot(o_scr[...], wo_ref[...], preferred_element_type=F32)
    o_ref[...] = x + _rms_scale(y, g2_ref[...])


def _xattn_call(l, x, g1, wq, k_all, v_all, wo, g2, n_mem):
    B, T, D = x.shape
    lay = lambda b, j: (l, 0, 0)
    vec = pl.BlockSpec((None, 1, D), lay)
    kern = functools.partial(_xattn_kernel, head_dim=D // XATTN_HEADS)
    return pl.pallas_call(
        kern,
        grid=(B, T // TS),
        in_specs=[
            pl.BlockSpec((None, TS, D), lambda b, j: (b, j, 0)),
            vec,
            _resident((None, D, D), lay),
            pl.BlockSpec((None, n_mem, D), lambda b, j: (l, b, 0)),
            pl.BlockSpec((None, n_mem, D), lambda b, j: (l, b, 0)),
            _resident((None, D, D), lay),
            vec,
        ],
        out_specs=pl.BlockSpec((None, TS, D), lambda b, j: (b, j, 0)),
        out_shape=jax.ShapeDtypeStruct((B, T, D), F32),
        scratch_shapes=[pltpu.VMEM((TS, D), BF16)],
        compiler_params=pltpu.CompilerParams(
            dimension_semantics=("arbitrary", "arbitrary"), vmem_limit_bytes=VMEM_LIMIT),
        name="xattn",
    )(x, g1, wq, k_all, v_all, wo, g2)


def _ffn_kernel(x_ref, g1_ref, wup_ref, cw_ref, wdn_ref, g2_ref, o_ref,
                h_scr, ua_scr, ub_scr, hal_scr, a_scr, *, n_chunks):
    j = pl.program_id(1)
    tiles = FF_CHUNK // LANES

    @pl.when(j == 0)
    def _zero_halo():
        hal_scr[...] = jnp.zeros(hal_scr.shape, F32)

    h_scr[...] = _rms_scale(x_ref[...], g1_ref[...]).astype(BF16)

    def up(c, u_scr):
        for half in range(2):
            wi = c + half * n_chunks
            u = jnp.dot(h_scr[...], wup_ref[wi], preferred_element_type=F32)
            for i in range(tiles):
                ct = wi * tiles + i
                s = half * tiles + i
                u_scr[s, 0:FFN_HALO, :] = hal_scr[ct]
                u_scr[s, FFN_HALO:FFN_HALO + TS, :] = u[:, i * LANES:(i + 1) * LANES]
                hal_scr[ct] = u[TS - FFN_HALO:TS, i * LANES:(i + 1) * LANES]

    def act(c, u_scr):
        for i in range(tiles):
            taps = []
            for ct in (c * tiles + i, (c + n_chunks) * tiles + i):
                w = cw_ref[ct]
                taps.append([jnp.broadcast_to(w[FFN_CONV_KERNEL - 1 - k:FFN_CONV_KERNEL - k, :], (RC, LANES))
                             for k in range(FFN_CONV_KERNEL)])
            col = c * FF_CHUNK + i * LANES
            if not isinstance(col, int):
                col = pl.multiple_of(col, LANES)
            for r0 in range(0, TS, RC):
                gv = []
                for s, w in ((i, taps[0]), (tiles + i, taps[1])):
                    base = r0 + FFN_HALO
                    acc = w[0] * u_scr[s, base:base + RC, :]
                    for k in range(1, FFN_CONV_KERNEL):
                        acc = acc + w[k] * u_scr[s, base - k:base - k + RC, :]
                    gv.append(acc)
                a = gv[0] * _sigmoid(gv[0]) * gv[1]
                a_scr[r0:r0 + RC, pl.ds(col, LANES)] = a.astype(BF16)

    assert n_chunks % 2 == 1
    up(0, ua_scr)

    def pair(p, carry):
        c = 2 * p
        up(c + 1, ub_scr)
        act(c, ua_scr)
        up(c + 2, ua_scr)
        act(c + 1, ub_scr)
        return carry

    lax.fori_loop(0, (n_chunks - 1) // 2, pair, 0)
    act(n_chunks - 1, ua_scr)

    y = jnp.dot(a_scr[...], wdn_ref[...], preferred_element_type=F32)
    o_ref[...] = x_ref[...] + _rms_scale(y, g2_ref[...])


def _ffn_call(l, x, g1, w_up, cw, w_dn, g2):
    B, T, D = x.shape
    n_up = w_up.shape[1]
    n_chunks = n_up // 2
    n_ct = cw.shape[1]
    lay3 = lambda b, j: (l, 0, 0)
    lay4 = lambda b, j: (l, 0, 0, 0)
    vec = pl.BlockSpec((None, 1, D), lay3)
    kern = functools.partial(_ffn_kernel, n_chunks=n_chunks)
    return pl.pallas_call(
        kern,
        grid=(B, T // TS),
        in_specs=[
            pl.BlockSpec((None, TS, D), lambda b, j: (b, j, 0)),
            vec,
            _resident((None, n_up, D, FF_CHUNK), lay4),
            pl.BlockSpec((None, n_ct, FFN_CONV_KERNEL, LANES), lay4),
            _resident((None, n_chunks * FF_CHUNK, D), lay3),
            vec,
        ],
        out_specs=pl.BlockSpec((None, TS, D), lambda b, j: (b, j, 0)),
        out_shape=jax.ShapeDtypeStruct((B, T, D), F32),
        scratch_shapes=[
            pltpu.VMEM((TS, D), BF16),
            pltpu.VMEM((2 * FF_CHUNK // LANES, FFN_HALO + TS, LANES), F32),
            pltpu.VMEM((2 * FF_CHUNK // LANES, FFN_HALO + TS, LANES), F32),
            pltpu.VMEM((n_ct, FFN_HALO, LANES), F32),
            pltpu.VMEM((TS, n_chunks * FF_CHUNK), BF16),
        ],
        compiler_params=pltpu.CompilerParams(
            dimension_semantics=("arbitrary", "arbitrary"), vmem_limit_bytes=VMEM_LIMIT),
        name="ffn",
    )(x, g1, w_up, cw, w_dn, g2)


def _block_diag(maps):
    L, G, P, _ = maps.shape
    eye = jnp.eye(G, dtype=maps.dtype)
    return jnp.einsum("lgpq,gh->lgphq", maps, eye).reshape(L, G * P, G * P)


def kernel(x, mem, mem_norm, mix_pre_norm, mix_post_norm, w_in, pool_maps, pool_scale, conf_dw_w, conf_dw_b, conf_ln_g, conf_ln_b, sconv_w, w_out, xattn_pre_norm, xattn_post_norm, xattn_wq, xattn_wk, xattn_wv, xattn_wo, ffn_pre_norm, ffn_post_norm, ffn_w_up, ffn_conv_w, ffn_w_down):
    B, T, D = x.shape
    L = w_in.shape[0]
    n_mem = mem.shape[1]
    d_ff = ffn_w_down.shape[1]
    assert T % TS == 0 and d_ff % FF_CHUNK == 0 and D % LANES == 0

    row = lambda p: p.reshape(L, 1, p.shape[-1])
    bf = lambda w: w.astype(BF16)

    n_chunks = d_ff // FF_CHUNK
    w_up_c = bf(ffn_w_up).reshape(L, D, 2 * n_chunks, FF_CHUNK).transpose(0, 2, 1, 3)
    w_dn_b = bf(ffn_w_down)
    ffn_cw = ffn_conv_w.reshape(L, FFN_CONV_KERNEL, 2 * d_ff // LANES, LANES).transpose(0, 2, 1, 3)
    pmap = bf(_block_diag(pool_maps))

    k_all, v_all = _kv_call(mem.reshape(B * n_mem, D), mem_norm.reshape(1, D), bf(xattn_wk), bf(xattn_wv))
    w_in_b, w_out_b, wq_b, wo_b = bf(w_in), bf(w_out), bf(xattn_wq), bf(xattn_wo)

    for l in range(L):
        x = _mixer_call(l, x, row(mix_pre_norm), w_in_b, pmap, row(pool_scale), conf_dw_w,
                        row(conf_dw_b), row(conf_ln_g), row(conf_ln_b), sconv_w, w_out_b,
                        row(mix_post_norm))
        x = _xattn_call(l, x, row(xattn_pre_norm), wq_b, k_all, v_all, wo_b, row(xattn_post_norm), n_mem)
        x = _ffn_call(l, x, row(ffn_pre_norm), w_up_c, ffn_cw, w_dn_b, row(ffn_post_norm))
    return x
```

```python
import functools

import jax
import jax.numpy as jnp
from jax import lax
from jax.experimental import pallas as pl
from jax.experimental.pallas import tpu as pltpu

F32 = jnp.float32
BF16 = jnp.bfloat16

EPS = 1e-6
LANES = 128
POOL_WINDOWS = (2, 4, 8, 16)
CONF_KERNEL = 31
SCONV_KERNEL = 3
FFN_CONV_KERNEL = 3
XATTN_HEADS = 4

TS = 512
HALO = 32
FFN_HALO = 8
RC = 64
MXU_COLS = 256
MIX_ROWS = 64
MIX_RC = 32
GATE_ROWS = 128
FF_CHUNK = 256
VMEM_LIMIT = 56 * 1024 * 1024


def _rms_scale(x, g):
    ms = jnp.mean(x * x, axis=-1, keepdims=True)
    return x * lax.rsqrt(ms + EPS) * g


def _sigmoid(x):
    return 1.0 / (1.0 + jnp.exp(-x))


def _resident(shape, index_map):
    return pl.BlockSpec(shape, index_map, pipeline_mode=pl.Buffered(1))


def _kv_kernel(mem_ref, g_ref, wk_ref, wv_ref, k_ref, v_ref):
    mn = _rms_scale(mem_ref[...], g_ref[...]).astype(BF16)
    k_ref[...] = jnp.dot(mn, wk_ref[...], preferred_element_type=F32).astype(BF16)
    v_ref[...] = jnp.dot(mn, wv_ref[...], preferred_element_type=F32).astype(BF16)


def _kv_call(mem2d, mem_norm, wk, wv):
    L, D, _ = wk.shape
    R = mem2d.shape[0]
    return pl.pallas_call(
        _kv_kernel,
        grid=(L,),
        in_specs=[
            pl.BlockSpec((R, D), lambda l: (0, 0)),
            pl.BlockSpec((1, D), lambda l: (0, 0)),
            pl.BlockSpec((None, D, D), lambda l: (l, 0, 0)),
            pl.BlockSpec((None, D, D), lambda l: (l, 0, 0)),
        ],
        out_specs=[
            pl.BlockSpec((None, R, D), lambda l: (l, 0, 0)),
            pl.BlockSpec((None, R, D), lambda l: (l, 0, 0)),
        ],
        out_shape=[jax.ShapeDtypeStruct((L, R, D), BF16)] * 2,
        compiler_params=pltpu.CompilerParams(
            dimension_semantics=("arbitrary",), vmem_limit_bytes=VMEM_LIMIT),
        name="kv_proj",
    )(mem2d, mem_norm, wk, wv)


def _mixer_kernel(xc_ref, xp_ref, g1_ref, win_ref, pmap_ref, psc_ref, cw_ref, cb_ref, lg_ref, lb_ref,
                  sw_ref, wout_ref, g2_ref, o_ref,
                  h_scr, yo_scr, z0, v0, c0, y0, z1, v1, c1, y1,
                  *, d_pool, d_conf, d_sconv, tiles_per_seq):
    s = pl.program_id(0)
    n_p, n_c, n_s = d_pool // LANES, d_conf // LANES, d_sconv // LANES
    t_a, t_g = n_p, n_p + n_c
    t_b = n_p + 2 * n_c
    t_c, t_x = t_b + n_s, t_b + 2 * n_s
    d_in = win_ref.shape[-1]
    n_slices = d_in // MXU_COLS
    n_out_slices = wout_ref.shape[-1] // MXU_COLS
    assert n_slices * MIX_ROWS == TS and n_out_slices <= n_slices
    assert d_in - n_slices * MXU_COLS in (0, LANES)

    @pl.when(s == 0)
    def _init():
        for ref in (z0, v0, c0, y0, z1, v1, c1, y1):
            ref[...] = jnp.zeros(ref.shape, ref.dtype)

    def mix_rows(src, y_scr, t0, r0):
        z_scr, v_scr, c_scr = src
        lane = lax.broadcasted_iota(jnp.int32, (MIX_RC, LANES), 1)
        row = lax.broadcasted_iota(jnp.int32, (MIX_RC, LANES), 0)
        low = lane < (LANES // 2)

        for sub in range(0, MIX_ROWS, MIX_RC):
            rr = r0 + sub

            def shifted(scr, i, k, rr=rr):
                return scr[i, pl.ds(rr + (HALO - k), MIX_RC), :]

            rows = pl.ds(rr, MIX_RC)
            pos1 = t0 + rr + 1 + row
            for i in range(n_p):
                w_small, w_big = POOL_WINDOWS[2 * i], POOL_WINDOWS[2 * i + 1]
                u0 = shifted(z_scr, i, 0)
                s_small = u0
                for k in range(1, w_small):
                    s_small = s_small + shifted(z_scr, i, k)
                s_big = s_small
                for k in range(w_small, w_big):
                    s_big = s_big + shifted(z_scr, i, k)
                cnt = jnp.minimum(pos1, jnp.where(low, w_small, w_big)).astype(F32)
                pooled = jnp.where(low, s_small, s_big) / cnt - u0
                y_scr[rows, i * LANES:(i + 1) * LANES] = pooled.astype(BF16)

            conv = []
            for i in range(n_c):
                cs = slice(i * LANES, (i + 1) * LANES)
                acc = cb_ref[:, cs] + cw_ref[CONF_KERNEL - 1:CONF_KERNEL, cs] * shifted(v_scr, i, 0)
                for k in range(1, CONF_KERNEL):
                    acc = acc + cw_ref[CONF_KERNEL - 1 - k:CONF_KERNEL - k, cs] * shifted(v_scr, i, k)
                conv.append(acc)
            tot = conv[0]
            for i in range(1, n_c):
                tot = tot + conv[i]
            mu = jnp.sum(tot, axis=-1, keepdims=True) * (1.0 / d_conf)
            cen = [c - mu for c in conv]
            sq = cen[0] * cen[0]
            for i in range(1, n_c):
                sq = sq + cen[i] * cen[i]
            rstd = lax.rsqrt(jnp.sum(sq, axis=-1, keepdims=True) * (1.0 / d_conf) + EPS)
            for i in range(n_c):
                cs = slice(i * LANES, (i + 1) * LANES)
                yv = cen[i] * rstd * lg_ref[:, cs] + lb_ref[:, cs]
                y_scr[rows, d_pool + i * LANES: d_pool + (i + 1) * LANES] = (yv * _sigmoid(yv)).astype(BF16)

            for i in range(n_s):
                cs = slice(i * LANES, (i + 1) * LANES)
                acc = sw_ref[SCONV_KERNEL - 1:SCONV_KERNEL, cs] * shifted(c_scr, i, 0)
                for k in range(1, SCONV_KERNEL):
                    acc = acc + sw_ref[SCONV_KERNEL - 1 - k:SCONV_KERNEL - k, cs] * shifted(c_scr, i, k)
                gate_b = z_scr[t_b + i, pl.ds(rr + HALO, MIX_RC), :]
                off = d_pool + d_conf + i * LANES
                y_scr[rows, off:off + LANES] = (gate_b * acc).astype(BF16)

        stored = pltpu.bitcast(y_scr[pl.ds(r0, MIX_ROWS), :], F32)
        bits = lax.bitcast_convert_type(jnp.max(stored), jnp.int32)
        return lax.shift_right_logical(lax.shift_right_logical(bits, 16), 16)

    def phase(x_rows, xp_rows, out_rows, dst, src, y_mix, y_proj, tile_mix, seq_start):
        z_d, v_d, c_d = dst
        z_s, v_s, c_s = src

        def halo_from(d_ref, s_ref, n):
            tail = s_ref[0:n, TS:TS + HALO, :]
            if seq_start is not None:
                tail = jnp.where(seq_start, 0.0, tail)
            d_ref[0:n, 0:HALO, :] = tail

        h_scr[...] = _rms_scale(xc_ref[x_rows, :], g1_ref[...]).astype(BF16)
        ya = jnp.dot(y_proj[:, 0:d_pool], pmap_ref[...], preferred_element_type=F32) * psc_ref[...]
        y_proj[:, 0:d_pool] = ya.astype(BF16)
        halo_from(z_d, z_s, n_p)
        t0 = (tile_mix % tiles_per_seq) * TS

        turn = 0
        for i in range(n_slices):
            col = i * MXU_COLS + turn
            if i > 0:
                col = pl.multiple_of(col, MXU_COLS)
            z = jnp.dot(h_scr[...], win_ref[:, pl.ds(col, MXU_COLS)], preferred_element_type=F32)
            for q in range(MXU_COLS // LANES):
                z_d[(MXU_COLS // LANES) * i + q, HALO:HALO + TS, :] = z[:, q * LANES:(q + 1) * LANES]
            if i < n_out_slices:
                yo_scr[:, i * MXU_COLS:(i + 1) * MXU_COLS] = jnp.dot(
                    y_proj[...], wout_ref[:, pl.ds(col, MXU_COLS)], preferred_element_type=F32)
            turn = mix_rows(src, y_mix, t0, i * MIX_ROWS)

        if d_in > n_slices * MXU_COLS:
            col = n_slices * MXU_COLS
            z = jnp.dot(h_scr[...], win_ref[:, col:col + LANES], preferred_element_type=F32)
            z_d[col // LANES, HALO:HALO + TS, :] = z
        halo_from(v_d, v_s, n_c)
        halo_from(c_d, c_s, n_s)
        for r0 in range(0, TS, GATE_ROWS):
            body = slice(HALO + r0, HALO + r0 + GATE_ROWS)
            for i in range(n_c):
                v_d[i, body, :] = z_d[t_a + i, body, :] * _sigmoid(z_d[t_g + i, body, :])
            for i in range(n_s):
                c_d[i, body, :] = z_d[t_c + i, body, :] * z_d[t_x + i, body, :]
        o_ref[out_rows, :] = xp_ref[xp_rows, :] + _rms_scale(yo_scr[...], g2_ref[...])

    even, odd = slice(0, TS), slice(TS, 2 * TS)
    seq_start = (s % (tiles_per_seq // 2)) == 0
    set0, set1 = (z0, v0, c0), (z1, v1, c1)
    phase(even, even, even, set0, set1, y1, y0, 2 * s - 1, seq_start)
    phase(odd, odd, odd, set1, set0, y0, y1, 2 * s, None)


def _mixer_call(l, x, g1, w_in, pmap, psc, cw, cb, lg, lb, sw, w_out, g2):
    B, T, D = x.shape
    d_in = w_in.shape[-1]
    d_pool = pmap.shape[-1]
    d_conf = cw.shape[-1]
    d_sconv = sw.shape[-1]
    d_mix = w_out.shape[1]
    tiles_per_seq = T // TS
    assert tiles_per_seq % 2 == 0
    n_pairs = B * tiles_per_seq // 2
    lay = lambda s: (l, 0, 0)
    vec = lambda c: pl.BlockSpec((None, 1, c), lay)
    kern = functools.partial(_mixer_kernel, d_pool=d_pool, d_conf=d_conf, d_sconv=d_sconv,
                             tiles_per_seq=tiles_per_seq)
    staging = [
        pltpu.VMEM((d_in // LANES, HALO + TS, LANES), F32),
        pltpu.VMEM((d_conf // LANES, HALO + TS, LANES), F32),
        pltpu.VMEM((d_sconv // LANES, HALO + TS, LANES), F32),
        pltpu.VMEM((TS, d_mix), BF16),
    ]
    x2 = x.reshape(B * T, D)
    out = pl.pallas_call(
        kern,
        grid=(n_pairs + 1,),
        in_specs=[
            pl.BlockSpec((2 * TS, D), lambda s: (jnp.minimum(s, n_pairs - 1), 0)),
            pl.BlockSpec((2 * TS, D), lambda s: (jnp.maximum(s - 1, 0), 0)),
            vec(D),
            _resident((None, D, d_in), lay),
            _resident((None, d_pool, d_pool), lay),
            vec(d_pool),
            pl.BlockSpec((None, CONF_KERNEL, d_conf), lay),
            vec(d_conf), vec(d_conf), vec(d_conf),
            pl.BlockSpec((None, SCONV_KERNEL, d_sconv), lay),
            _resident((None, d_mix, D), lay),
            vec(D),
        ],
        out_specs=pl.BlockSpec((2 * TS, D), lambda s: (jnp.maximum(s - 1, 0), 0)),
        out_shape=jax.ShapeDtypeStruct((B * T, D), F32),
        scratch_shapes=[pltpu.VMEM((TS, D), BF16), pltpu.VMEM((TS, D), F32)] + staging + staging,
        compiler_params=pltpu.CompilerParams(
            dimension_semantics=("arbitrary",), vmem_limit_bytes=VMEM_LIMIT),
        name="mixer",
    )(x2, x2, g1, w_in, pmap, psc, cw, cb, lg, lb, sw, w_out, g2)
    return out.reshape(B, T, D)


def _xattn_kernel(x_ref, g1_ref, wq_ref, k_ref, v_ref, wo_ref, g2_ref, o_ref, o_scr, *, head_dim):
    x = x_ref[...]
    h = _rms_scale(x, g1_ref[...]).astype(BF16)
    q = (jnp.dot(h, wq_ref[...], preferred_element_type=F32) * (head_dim ** -0.5)).astype(BF16)
    for hd in range(XATTN_HEADS):
        sl = slice(hd * head_dim, (hd + 1) * head_dim)
        s = lax.dot_general(q[:, sl], k_ref[:, sl], (((1,), (1,)), ((), ())),
                            preferred_element_type=F32)
        m = jnp.max(s, axis=-1, keepdims=True)
        p = jnp.exp(s - m)
        inv_l = 1.0 / jnp.sum(p, axis=-1, keepdims=True)
        o = jnp.dot(p.astype(BF16), v_ref[:, sl], preferred_element_type=F32) * inv_l
        o_scr[:, sl] = o.astype(BF16)
    y = jnp.dot(o_scr[...], wo_ref[...], preferred_element_type=F32)
    o_ref[...] = x + _rms_scale(y, g2_ref[...])


def _xattn_call(l, x, g1, wq, k_all, v_all, wo, g2, n_mem):
    B, T, D = x.shape
    lay = lambda b, j: (l, 0, 0)
    vec = pl.BlockSpec((None, 1, D), lay)
    kern = functools.partial(_xattn_kernel, head_dim=D // XATTN_HEADS)
    return pl.pallas_call(
        kern,
        grid=(B, T // TS),
        in_specs=[
            pl.BlockSpec((None, TS, D), lambda b, j: (b, j, 0)),
            vec,
            _resident((None, D, D), lay),
            pl.BlockSpec((None, n_mem, D), lambda b, j: (l, b, 0)),
            pl.BlockSpec((None, n_mem, D), lambda b, j: (l, b, 0)),
            _resident((None, D, D), lay),
            vec,
        ],
        out_specs=pl.BlockSpec((None, TS, D), lambda b, j: (b, j, 0)),
        out_shape=jax.ShapeDtypeStruct((B, T, D), F32),
        scratch_shapes=[pltpu.VMEM((TS, D), BF16)],
        compiler_params=pltpu.CompilerParams(
            dimension_semantics=("arbitrary", "arbitrary"), vmem_limit_bytes=VMEM_LIMIT),
        name="xattn",
    )(x, g1, wq, k_all, v_all, wo, g2)


def _ffn_kernel(x_ref, g1_ref, wup_ref, cw_ref, wdn_ref, g2_ref, o_ref,
                h_scr, ua_scr, ub_scr, hal_scr, a_scr, *, n_chunks):
    j = pl.program_id(1)
    tiles = FF_CHUNK // LANES

    @pl.when(j == 0)
    def _zero_halo():
        hal_scr[...] = jnp.zeros(hal_scr.shape, F32)

    h_scr[...] = _rms_scale(x_ref[...], g1_ref[...]).astype(BF16)

    def up(c, u_scr):
        for half in range(2):
            wi = c + half * n_chunks
            col = wi * FF_CHUNK
            if not isinstance(col, int):
                col = pl.multiple_of(col, FF_CHUNK)
            u = jnp.dot(h_scr[...], wup_ref[:, pl.ds(col, FF_CHUNK)],
                        preferred_element_type=F32)
            for i in range(tiles):
                ct = wi * tiles + i
                s = half * tiles + i
                u_scr[s, 0:FFN_HALO, :] = hal_scr[ct]
                u_scr[s, FFN_HALO:FFN_HALO + TS, :] = u[:, i * LANES:(i + 1) * LANES]
                hal_scr[ct] = u[TS - FFN_HALO:TS, i * LANES:(i + 1) * LANES]

    def act(c, u_scr):
        for i in range(tiles):
            taps = []
            for ct in (c * tiles + i, (c + n_chunks) * tiles + i):
                w = cw_ref[ct]
                taps.append([jnp.broadcast_to(w[FFN_CONV_KERNEL - 1 - k:FFN_CONV_KERNEL - k, :], (RC, LANES))
                             for k in range(FFN_CONV_KERNEL)])
            col = c * FF_CHUNK + i * LANES
            if not isinstance(col, int):
                col = pl.multiple_of(col, LANES)
            for r0 in range(0, TS, RC):
                gv = []
                for s, w in ((i, taps[0]), (tiles + i, taps[1])):
                    base = r0 + FFN_HALO
                    acc = w[0] * u_scr[s, base:base + RC, :]
                    for k in range(1, FFN_CONV_KERNEL):
                        acc = acc + w[k] * u_scr[s, base - k:base - k + RC, :]
                    gv.append(acc)
                a = gv[0] * _sigmoid(gv[0]) * gv[1]
                a_scr[r0:r0 + RC, pl.ds(col, LANES)] = a.astype(BF16)

    assert n_chunks % 2 == 1
    up(0, ua_scr)

    def pair(p, carry):
        c = 2 * p
        up(c + 1, ub_scr)
        act(c, ua_scr)
        up(c + 2, ua_scr)
        act(c + 1, ub_scr)
        return carry

    lax.fori_loop(0, (n_chunks - 1) // 2, pair, 0)
    act(n_chunks - 1, ua_scr)

    y = jnp.dot(a_scr[...], wdn_ref[...], preferred_element_type=F32)
    o_ref[...] = x_ref[...] + _rms_scale(y, g2_ref[...])


def _ffn_call(l, x, g1, w_up, cw, w_dn, g2):
    B, T, D = x.shape
    n_chunks = w_dn.shape[1] // FF_CHUNK
    n_ct = cw.shape[1]
    lay3 = lambda b, j: (l, 0, 0)
    lay4 = lambda b, j: (l, 0, 0, 0)
    vec = pl.BlockSpec((None, 1, D), lay3)
    kern = functools.partial(_ffn_kernel, n_chunks=n_chunks)
    return pl.pallas_call(
        kern,
        grid=(B, T // TS),
        in_specs=[
            pl.BlockSpec((None, TS, D), lambda b, j: (b, j, 0)),
            vec,
            _resident((None, D, 2 * n_chunks * FF_CHUNK), lay3),
            pl.BlockSpec((None, n_ct, FFN_CONV_KERNEL, LANES), lay4),
            _resident((None, n_chunks * FF_CHUNK, D), lay3),
            vec,
        ],
        out_specs=pl.BlockSpec((None, TS, D), lambda b, j: (b, j, 0)),
        out_shape=jax.ShapeDtypeStruct((B, T, D), F32),
        scratch_shapes=[
            pltpu.VMEM((TS, D), BF16),
            pltpu.VMEM((2 * FF_CHUNK // LANES, FFN_HALO + TS, LANES), F32),
            pltpu.VMEM((2 * FF_CHUNK // LANES, FFN_HALO + TS, LANES), F32),
            pltpu.VMEM((n_ct, FFN_HALO, LANES), F32),
            pltpu.VMEM((TS, n_chunks * FF_CHUNK), BF16),
        ],
        compiler_params=pltpu.CompilerParams(
            dimension_semantics=("arbitrary", "arbitrary"), vmem_limit_bytes=VMEM_LIMIT),
        name="ffn",
    )(x, g1, w_up, cw, w_dn, g2)


def _block_diag(maps):
    L, G, P, _ = maps.shape
    eye = jnp.eye(G, dtype=maps.dtype)
    return jnp.einsum("lgpq,gh->lgphq", maps, eye).reshape(L, G * P, G * P)


def kernel(x, mem, mem_norm, mix_pre_norm, mix_post_norm, w_in, pool_maps, pool_scale, conf_dw_w, conf_dw_b, conf_ln_g, conf_ln_b, sconv_w, w_out, xattn_pre_norm, xattn_post_norm, xattn_wq, xattn_wk, xattn_wv, xattn_wo, ffn_pre_norm, ffn_post_norm, ffn_w_up, ffn_conv_w, ffn_w_down):
    B, T, D = x.shape
    L = w_in.shape[0]
    n_mem = mem.shape[1]
    d_ff = ffn_w_down.shape[1]
    assert T % TS == 0 and d_ff % FF_CHUNK == 0 and D % LANES == 0

    row = lambda p: p.reshape(L, 1, p.shape[-1])
    bf = lambda w: w.astype(BF16)

    w_up_b = bf(ffn_w_up)
    w_dn_b = bf(ffn_w_down)
    ffn_cw = ffn_conv_w.reshape(L, FFN_CONV_KERNEL, 2 * d_ff // LANES, LANES).transpose(0, 2, 1, 3)
    pmap = bf(_block_diag(pool_maps))

    k_all, v_all = _kv_call(mem.reshape(B * n_mem, D), mem_norm.reshape(1, D), bf(xattn_wk), bf(xattn_wv))
    w_in_b, w_out_b, wq_b, wo_b = bf(w_in), bf(w_out), bf(xattn_wq), bf(xattn_wo)

    for l in range(L):
        x = _mixer_call(l, x, row(mix_pre_norm), w_in_b, pmap, row(pool_scale), conf_dw_w,
                        row(conf_dw_b), row(conf_ln_g), row(conf_ln_b), sconv_w, w_out_b,
                        row(mix_post_norm))
        x = _xattn_call(l, x, row(xattn_pre_norm), wq_b, k_all, v_all, wo_b, row(xattn_post_norm), n_mem)
        x = _ffn_call(l, x, row(ffn_pre_norm), w_up_b, ffn_cw, w_dn_b, row(ffn_post_norm))
    return x
```

```python
import functools

import jax
import jax.numpy as jnp
from jax import lax
from jax.experimental import pallas as pl
from jax.experimental.pallas import tpu as pltpu

F32 = jnp.float32
BF16 = jnp.bfloat16

EPS = 1e-6
LANES = 128
POOL_WINDOWS = (2, 4, 8, 16)
CONF_KERNEL = 31
SCONV_KERNEL = 3
FFN_CONV_KERNEL = 3
XATTN_HEADS = 4

TS = 512
HALO = 32
FFN_HALO = 8
RC = 64
MXU_COLS = 256
MIX_ROWS = 64
MIX_RC = 32
GATE_ROWS = 128
FF_CHUNK = 256
VMEM_LIMIT = 56 * 1024 * 1024


def _rms_scale(x, g):
    ms = jnp.mean(x * x, axis=-1, keepdims=True)
    return x * lax.rsqrt(ms + EPS) * g


def _sigmoid(x):
    return 1.0 / (1.0 + jnp.exp(-x))


def _resident(shape, index_map):
    return pl.BlockSpec(shape, index_map, pipeline_mode=pl.Buffered(1))


def _kv_kernel(mem_ref, g_ref, wk_ref, wv_ref, k_ref, v_ref):
    mn = _rms_scale(mem_ref[...], g_ref[...]).astype(BF16)
    k_ref[...] = jnp.dot(mn, wk_ref[...], preferred_element_type=F32).astype(BF16)
    v_ref[...] = jnp.dot(mn, wv_ref[...], preferred_element_type=F32).astype(BF16)


def _kv_call(mem2d, mem_norm, wk, wv):
    L, D, _ = wk.shape
    R = mem2d.shape[0]
    return pl.pallas_call(
        _kv_kernel,
        grid=(L,),
        in_specs=[
            pl.BlockSpec((R, D), lambda l: (0, 0)),
            pl.BlockSpec((1, D), lambda l: (0, 0)),
            pl.BlockSpec((None, D, D), lambda l: (l, 0, 0)),
            pl.BlockSpec((None, D, D), lambda l: (l, 0, 0)),
        ],
        out_specs=[
            pl.BlockSpec((None, R, D), lambda l: (l, 0, 0)),
            pl.BlockSpec((None, R, D), lambda l: (l, 0, 0)),
        ],
        out_shape=[jax.ShapeDtypeStruct((L, R, D), BF16)] * 2,
        compiler_params=pltpu.CompilerParams(
            dimension_semantics=("arbitrary",), vmem_limit_bytes=VMEM_LIMIT),
        name="kv_proj",
    )(mem2d, mem_norm, wk, wv)


def _mixer_kernel(xc_ref, xp_ref, g1_ref, win_ref, pmap_ref, psc_ref, cw_ref, cb_ref, lg_ref, lb_ref,
                  sw_ref, wout_ref, g2_ref, o_ref,
                  h_scr, yo_scr, z0, v0, c0, y0, z1, v1, c1, y1,
                  *, d_pool, d_conf, d_sconv, tiles_per_seq):
    s = pl.program_id(0)
    n_p, n_c, n_s = d_pool // LANES, d_conf // LANES, d_sconv // LANES
    t_a, t_g = n_p, n_p + n_c
    t_b = n_p + 2 * n_c
    t_c, t_x = t_b + n_s, t_b + 2 * n_s
    d_in = win_ref.shape[-1]
    n_slices = d_in // MXU_COLS
    n_out_slices = wout_ref.shape[-1] // MXU_COLS
    assert n_slices * MIX_ROWS == TS and n_out_slices <= n_slices
    assert d_in - n_slices * MXU_COLS in (0, LANES)

    @pl.when(s == 0)
    def _init():
        for ref in (z0, v0, c0, y0, z1, v1, c1, y1):
            ref[...] = jnp.zeros(ref.shape, ref.dtype)

    def mix_rows(src, y_scr, t0, r0):
        z_scr, v_scr, c_scr = src
        lane = lax.broadcasted_iota(jnp.int32, (MIX_RC, LANES), 1)
        row = lax.broadcasted_iota(jnp.int32, (MIX_RC, LANES), 0)
        low = lane < (LANES // 2)

        for sub in range(0, MIX_ROWS, MIX_RC):
            rr = r0 + sub

            def shifted(scr, i, k, rr=rr):
                return scr[i, pl.ds(rr + (HALO - k), MIX_RC), :]

            rows = pl.ds(rr, MIX_RC)
            pos1 = t0 + rr + 1 + row
            for i in range(n_p):
                w_small, w_big = POOL_WINDOWS[2 * i], POOL_WINDOWS[2 * i + 1]
                u0 = shifted(z_scr, i, 0)
                s_small = u0
                for k in range(1, w_small):
                    s_small = s_small + shifted(z_scr, i, k)
                s_big = s_small
                for k in range(w_small, w_big):
                    s_big = s_big + shifted(z_scr, i, k)
                cnt = jnp.minimum(pos1, jnp.where(low, w_small, w_big)).astype(F32)
                pooled = jnp.where(low, s_small, s_big) / cnt - u0
                y_scr[rows, i * LANES:(i + 1) * LANES] = pooled.astype(BF16)

            conv = []
            for i in range(n_c):
                cs = slice(i * LANES, (i + 1) * LANES)
                acc = cb_ref[:, cs] + cw_ref[CONF_KERNEL - 1:CONF_KERNEL, cs] * shifted(v_scr, i, 0)
                for k in range(1, CONF_KERNEL):
                    acc = acc + cw_ref[CONF_KERNEL - 1 - k:CONF_KERNEL - k, cs] * shifted(v_scr, i, k)
                conv.append(acc)
            tot = conv[0]
            for i in range(1, n_c):
                tot = tot + conv[i]
            mu = jnp.sum(tot, axis=-1, keepdims=True) * (1.0 / d_conf)
            cen = [c - mu for c in conv]
            sq = cen[0] * cen[0]
            for i in range(1, n_c):
                sq = sq + cen[i] * cen[i]
            rstd = lax.rsqrt(jnp.sum(sq, axis=-1, keepdims=True) * (1.0 / d_conf) + EPS)
            for i in range(n_c):
                cs = slice(i * LANES, (i + 1) * LANES)
                yv = cen[i] * rstd * lg_ref[:, cs] + lb_ref[:, cs]
                y_scr[rows, d_pool + i * LANES: d_pool + (i + 1) * LANES] = (yv * _sigmoid(yv)).astype(BF16)

            for i in range(n_s):
                cs = slice(i * LANES, (i + 1) * LANES)
                acc = sw_ref[SCONV_KERNEL - 1:SCONV_KERNEL, cs] * shifted(c_scr, i, 0)
                for k in range(1, SCONV_KERNEL):
                    acc = acc + sw_ref[SCONV_KERNEL - 1 - k:SCONV_KERNEL - k, cs] * shifted(c_scr, i, k)
                gate_b = z_scr[t_b + i, pl.ds(rr + HALO, MIX_RC), :]
                off = d_pool + d_conf + i * LANES
                y_scr[rows, off:off + LANES] = (gate_b * acc).astype(BF16)

        stored = pltpu.bitcast(y_scr[pl.ds(r0, MIX_ROWS), :], F32)
        bits = lax.bitcast_convert_type(jnp.max(stored), jnp.int32)
        return lax.shift_right_logical(lax.shift_right_logical(bits, 16), 16)

    def phase(x_rows, xp_rows, out_rows, dst, src, y_mix, y_proj, tile_mix, seq_start):
        z_d, v_d, c_d = dst
        z_s, v_s, c_s = src

        def halo_from(d_ref, s_ref, n):
            tail = s_ref[0:n, TS:TS + HALO, :]
            if seq_start is not None:
                tail = jnp.where(seq_start, 0.0, tail)
            d_ref[0:n, 0:HALO, :] = tail

        h_scr[...] = _rms_scale(xc_ref[x_rows, :], g1_ref[...]).astype(BF16)
        ya = jnp.dot(y_proj[:, 0:d_pool], pmap_ref[...], preferred_element_type=F32) * psc_ref[...]
        y_proj[:, 0:d_pool] = ya.astype(BF16)
        halo_from(z_d, z_s, n_p)
        t0 = (tile_mix % tiles_per_seq) * TS

        turn = 0
        for i in range(n_slices):
            col = i * MXU_COLS + turn
            if i > 0:
                col = pl.multiple_of(col, MXU_COLS)
            z = jnp.dot(h_scr[...], win_ref[:, pl.ds(col, MXU_COLS)], preferred_element_type=F32)
            for q in range(MXU_COLS // LANES):
                z_d[(MXU_COLS // LANES) * i + q, HALO:HALO + TS, :] = z[:, q * LANES:(q + 1) * LANES]
            if i < n_out_slices:
                yo_scr[:, i * MXU_COLS:(i + 1) * MXU_COLS] = jnp.dot(
                    y_proj[...], wout_ref[:, pl.ds(col, MXU_COLS)], preferred_element_type=F32)
            turn = mix_rows(src, y_mix, t0, i * MIX_ROWS)

        if d_in > n_slices * MXU_COLS:
            col = n_slices * MXU_COLS
            z = jnp.dot(h_scr[...], win_ref[:, col:col + LANES], preferred_element_type=F32)
            z_d[col // LANES, HALO:HALO + TS, :] = z
        halo_from(v_d, v_s, n_c)
        halo_from(c_d, c_s, n_s)
        for r0 in range(0, TS, GATE_ROWS):
            body = slice(HALO + r0, HALO + r0 + GATE_ROWS)
            for i in range(n_c):
                v_d[i, body, :] = z_d[t_a + i, body, :] * _sigmoid(z_d[t_g + i, body, :])
            for i in range(n_s):
                c_d[i, body, :] = z_d[t_c + i, body, :] * z_d[t_x + i, body, :]
        o_ref[out_rows, :] = xp_ref[xp_rows, :] + _rms_scale(yo_scr[...], g2_ref[...])

    even, odd = slice(0, TS), slice(TS, 2 * TS)
    seq_start = (s % (tiles_per_seq // 2)) == 0
    set0, set1 = (z0, v0, c0), (z1, v1, c1)
    phase(even, even, even, set0, set1, y1, y0, 2 * s - 1, seq_start)
    phase(odd, odd, odd, set1, set0, y0, y1, 2 * s, None)


def _mixer_call(l, x, g1, w_in, pmap, psc, cw, cb, lg, lb, sw, w_out, g2):
    B, T, D = x.shape
    d_in = w_in.shape[-1]
    d_pool = pmap.shape[-1]
    d_conf = cw.shape[-1]
    d_sconv = sw.shape[-1]
    d_mix = w_out.shape[1]
    tiles_per_seq = T // TS
    assert tiles_per_seq % 2 == 0
    n_pairs = B * tiles_per_seq // 2
    lay = lambda s: (l, 0, 0)
    vec = lambda c: pl.BlockSpec((None, 1, c), lay)
    kern = functools.partial(_mixer_kernel, d_pool=d_pool, d_conf=d_conf, d_sconv=d_sconv,
                             tiles_per_seq=tiles_per_seq)
    staging = [
        pltpu.VMEM((d_in // LANES, HALO + TS, LANES), F32),
        pltpu.VMEM((d_conf // LANES, HALO + TS, LANES), F32),
        pltpu.VMEM((d_sconv // LANES, HALO + TS, LANES), F32),
        pltpu.VMEM((TS, d_mix), BF16),
    ]
    x2 = x.reshape(B * T, D)
    out = pl.pallas_call(
        kern,
        grid=(n_pairs + 1,),
        in_specs=[
            pl.BlockSpec((2 * TS, D), lambda s: (jnp.minimum(s, n_pairs - 1), 0)),
            pl.BlockSpec((2 * TS, D), lambda s: (jnp.maximum(s - 1, 0), 0)),
            vec(D),
            _resident((None, D, d_in), lay),
            _resident((None, d_pool, d_pool), lay),
            vec(d_pool),
            pl.BlockSpec((None, CONF_KERNEL, d_conf), lay),
            vec(d_conf), vec(d_conf), vec(d_conf),
            pl.BlockSpec((None, SCONV_KERNEL, d_sconv), lay),
            _resident((None, d_mix, D), lay),
            vec(D),
        ],
        out_specs=pl.BlockSpec((2 * TS, D), lambda s: (jnp.maximum(s - 1, 0), 0)),
        out_shape=jax.ShapeDtypeStruct((B * T, D), F32),
        scratch_shapes=[pltpu.VMEM((TS, D), BF16), pltpu.VMEM((TS, D), F32)] + staging + staging,
        compiler_params=pltpu.CompilerParams(
            dimension_semantics=("arbitrary",), vmem_limit_bytes=VMEM_LIMIT),
        name="mixer",
    )(x2, x2, g1, w_in, pmap, psc, cw, cb, lg, lb, sw, w_out, g2)
    return out.reshape(B, T, D)


XATTN_TS = 1024
FF_GROUP = 4


def _xattn_kernel(x_ref, g1_ref, wq_ref, k_ref, v_ref, wo_ref, g2_ref, o_ref, o_scr, *, head_dim):
    x = x_ref[...]
    h = _rms_scale(x, g1_ref[...]).astype(BF16)
    q = (jnp.dot(h, wq_ref[...], preferred_element_type=F32) * (head_dim ** -0.5)).astype(BF16)
    for hd in range(XATTN_HEADS):
        sl = slice(hd * head_dim, (hd + 1) * head_dim)
        s = lax.dot_general(q[:, sl], k_ref[:, sl], (((1,), (1,)), ((), ())),
                            preferred_element_type=F32)
        m = jnp.max(s, axis=-1, keepdims=True)
        p = jnp.exp(s - m)
        inv_l = 1.0 / jnp.sum(p, axis=-1, keepdims=True)
        o = jnp.dot(p.astype(BF16), v_ref[:, sl], preferred_element_type=F32) * inv_l
        o_scr[:, sl] = o.astype(BF16)
    y = jnp.dot(o_scr[...], wo_ref[...], preferred_element_type=F32)
    o_ref[...] = x + _rms_scale(y, g2_ref[...])


def _xattn_call(l, x, g1, wq, k_all, v_all, wo, g2, n_mem):
    B, T, D = x.shape
    lay = lambda b, j: (l, 0, 0)
    vec = pl.BlockSpec((None, 1, D), lay)
    kern = functools.partial(_xattn_kernel, head_dim=D // XATTN_HEADS)
    return pl.pallas_call(
        kern,
        grid=(B, T // XATTN_TS),
        in_specs=[
            pl.BlockSpec((None, XATTN_TS, D), lambda b, j: (b, j, 0)),
            vec,
            _resident((None, D, D), lay),
            pl.BlockSpec((None, n_mem, D), lambda b, j: (l, b, 0)),
            pl.BlockSpec((None, n_mem, D), lambda b, j: (l, b, 0)),
            _resident((None, D, D), lay),
            vec,
        ],
        out_specs=pl.BlockSpec((None, XATTN_TS, D), lambda b, j: (b, j, 0)),
        out_shape=jax.ShapeDtypeStruct((B, T, D), F32),
        scratch_shapes=[pltpu.VMEM((XATTN_TS, D), BF16)],
        compiler_params=pltpu.CompilerParams(
            dimension_semantics=("arbitrary", "arbitrary"), vmem_limit_bytes=VMEM_LIMIT),
        name="xattn",
    )(x, g1, wq, k_all, v_all, wo, g2)


def _ffn_kernel(x_ref, g1_ref, wup_ref, cw_ref, wdn_ref, g2_ref, o_ref,
                h_scr, ua_scr, ub_scr, hal_scr, a_scr, *, n_chunks):
    j = pl.program_id(1)
    tiles = FF_CHUNK // LANES

    @pl.when(j == 0)
    def _zero_halo():
        hal_scr[...] = jnp.zeros(hal_scr.shape, F32)

    h_scr[...] = _rms_scale(x_ref[...], g1_ref[...]).astype(BF16)

    def up(c, u_scr):
        for half in range(2):
            wi = c + half * n_chunks
            col = wi * FF_CHUNK
            if not isinstance(col, int):
                col = pl.multiple_of(col, FF_CHUNK)
            u = jnp.dot(h_scr[...], wup_ref[:, pl.ds(col, FF_CHUNK)],
                        preferred_element_type=F32)
            for i in range(tiles):
                ct = wi * tiles + i
                s = half * tiles + i
                u_scr[s, 0:FFN_HALO, :] = hal_scr[ct]
                u_scr[s, FFN_HALO:FFN_HALO + TS, :] = u[:, i * LANES:(i + 1) * LANES]
                hal_scr[ct] = u[TS - FFN_HALO:TS, i * LANES:(i + 1) * LANES]

    def act(c, u_scr):
        for i in range(tiles):
            taps = []
            for ct in (c * tiles + i, (c + n_chunks) * tiles + i):
                w = cw_ref[ct]
                taps.append([jnp.broadcast_to(w[FFN_CONV_KERNEL - 1 - k:FFN_CONV_KERNEL - k, :], (RC, LANES))
                             for k in range(FFN_CONV_KERNEL)])
            col = c * FF_CHUNK + i * LANES
            if not isinstance(col, int):
                col = pl.multiple_of(col, LANES)
            for r0 in range(0, TS, RC):
                gv = []
                for s, w in ((i, taps[0]), (tiles + i, taps[1])):
                    base = r0 + FFN_HALO
                    acc = w[0] * u_scr[s, base:base + RC, :]
                    for k in range(1, FFN_CONV_KERNEL):
                        acc = acc + w[k] * u_scr[s, base - k:base - k + RC, :]
                    gv.append(acc)
                a = gv[0] * _sigmoid(gv[0]) * gv[1]
                a_scr[r0:r0 + RC, pl.ds(col, LANES)] = a.astype(BF16)

    assert FF_GROUP % 2 == 0
    up(0, ua_scr)

    bufs = (ua_scr, ub_scr)
    n_groups = (n_chunks - 1) // FF_GROUP

    def group(p, carry):
        c0 = FF_GROUP * p
        for q in range(FF_GROUP):
            up(c0 + q + 1, bufs[(q + 1) % 2])
            act(c0 + q, bufs[q % 2])
        return carry

    lax.fori_loop(0, n_groups, group, 0)
    for c in range(FF_GROUP * n_groups, n_chunks):
        if c + 1 < n_chunks:
            up(c + 1, bufs[(c + 1) % 2])
        act(c, bufs[c % 2])

    y = jnp.dot(a_scr[...], wdn_ref[...], preferred_element_type=F32)
    o_ref[...] = x_ref[...] + _rms_scale(y, g2_ref[...])


def _ffn_call(l, x, g1, w_up, cw, w_dn, g2):
    B, T, D = x.shape
    n_chunks = w_dn.shape[1] // FF_CHUNK
    n_ct = cw.shape[1]
    lay3 = lambda b, j: (l, 0, 0)
    lay4 = lambda b, j: (l, 0, 0, 0)
    vec = pl.BlockSpec((None, 1, D), lay3)
    kern = functools.partial(_ffn_kernel, n_chunks=n_chunks)
    return pl.pallas_call(
        kern,
        grid=(B, T // TS),
        in_specs=[
            pl.BlockSpec((None, TS, D), lambda b, j: (b, j, 0)),
            vec,
            _resident((None, D, 2 * n_chunks * FF_CHUNK), lay3),
            pl.BlockSpec((None, n_ct, FFN_CONV_KERNEL, LANES), lay4),
            _resident((None, n_chunks * FF_CHUNK, D), lay3),
            vec,
        ],
        out_specs=pl.BlockSpec((None, TS, D), lambda b, j: (b, j, 0)),
        out_shape=jax.ShapeDtypeStruct((B, T, D), F32),
        scratch_shapes=[
            pltpu.VMEM((TS, D), BF16),
            pltpu.VMEM((2 * FF_CHUNK // LANES, FFN_HALO + TS, LANES), F32),
            pltpu.VMEM((2 * FF_CHUNK // LANES, FFN_HALO + TS, LANES), F32),
            pltpu.VMEM((n_ct, FFN_HALO, LANES), F32),
            pltpu.VMEM((TS, n_chunks * FF_CHUNK), BF16),
        ],
        compiler_params=pltpu.CompilerParams(
            dimension_semantics=("arbitrary", "arbitrary"), vmem_limit_bytes=VMEM_LIMIT),
        name="ffn",
    )(x, g1, w_up, cw, w_dn, g2)


def _block_diag(maps):
    L, G, P, _ = maps.shape
    eye = jnp.eye(G, dtype=maps.dtype)
    return jnp.einsum("lgpq,gh->lgphq", maps, eye).reshape(L, G * P, G * P)


def kernel(x, mem, mem_norm, mix_pre_norm, mix_post_norm, w_in, pool_maps, pool_scale, conf_dw_w, conf_dw_b, conf_ln_g, conf_ln_b, sconv_w, w_out, xattn_pre_norm, xattn_post_norm, xattn_wq, xattn_wk, xattn_wv, xattn_wo, ffn_pre_norm, ffn_post_norm, ffn_w_up, ffn_conv_w, ffn_w_down):
    B, T, D = x.shape
    L = w_in.shape[0]
    n_mem = mem.shape[1]
    d_ff = ffn_w_down.shape[1]
    assert T % TS == 0 and T % XATTN_TS == 0 and d_ff % FF_CHUNK == 0 and D % LANES == 0

    row = lambda p: p.reshape(L, 1, p.shape[-1])
    bf = lambda w: w.astype(BF16)

    w_up_b = bf(ffn_w_up)
    w_dn_b = bf(ffn_w_down)
    ffn_cw = ffn_conv_w.reshape(L, FFN_CONV_KERNEL, 2 * d_ff // LANES, LANES).transpose(0, 2, 1, 3)
    pmap = bf(_block_diag(pool_maps))

    k_all, v_all = _kv_call(mem.reshape(B * n_mem, D), mem_norm.reshape(1, D), bf(xattn_wk), bf(xattn_wv))
    w_in_b, w_out_b, wq_b, wo_b = bf(w_in), bf(w_out), bf(xattn_wq), bf(xattn_wo)

    for l in range(L):
        x = _mixer_call(l, x, row(mix_pre_norm), w_in_b, pmap, row(pool_scale), conf_dw_w,
                        row(conf_dw_b), row(conf_ln_g), row(conf_ln_b), sconv_w, w_out_b,
                        row(mix_post_norm))
        x = _xattn_call(l, x, row(xattn_pre_norm), wq_b, k_all, v_all, wo_b, row(xattn_post_norm), n_mem)
        x = _ffn_call(l, x, row(ffn_pre_norm), w_up_b, ffn_cw, w_dn_b, row(ffn_post_norm))
    return x
```

```python
import functools

import jax
import jax.numpy as jnp
from jax import lax
from jax.experimental import pallas as pl
from jax.experimental.pallas import tpu as pltpu

F32 = jnp.float32
BF16 = jnp.bfloat16

EPS = 1e-6
LANES = 128
POOL_WINDOWS = (2, 4, 8, 16)
CONF_KERNEL = 31
SCONV_KERNEL = 3
FFN_CONV_KERNEL = 3
XATTN_HEADS = 4

TS = 512
HALO = 32
FFN_HALO = 8
RC = 64
MXU_COLS = 256
MIX_TS = 256
MIX_ROWS = 32
MIX_RC = 32
GATE_ROWS = 128
FF_CHUNK = 256
VMEM_LIMIT = 56 * 1024 * 1024


def _rms_scale(x, g):
    ms = jnp.mean(x * x, axis=-1, keepdims=True)
    return x * lax.rsqrt(ms + EPS) * g


def _sigmoid(x):
    return 1.0 / (1.0 + jnp.exp(-x))


def _resident(shape, index_map):
    return pl.BlockSpec(shape, index_map, pipeline_mode=pl.Buffered(1))


def _kv_kernel(mem_ref, g_ref, wk_ref, wv_ref, k_ref, v_ref):
    mn = _rms_scale(mem_ref[...], g_ref[...]).astype(BF16)
    k_ref[...] = jnp.dot(mn, wk_ref[...].astype(BF16), preferred_element_type=F32).astype(BF16)
    v_ref[...] = jnp.dot(mn, wv_ref[...].astype(BF16), preferred_element_type=F32).astype(BF16)


def _kv_call(mem2d, mem_norm, wk, wv):
    L, D, _ = wk.shape
    R = mem2d.shape[0]
    return pl.pallas_call(
        _kv_kernel,
        grid=(L,),
        in_specs=[
            pl.BlockSpec((R, D), lambda l: (0, 0)),
            pl.BlockSpec((1, D), lambda l: (0, 0)),
            pl.BlockSpec((None, D, D), lambda l: (l, 0, 0)),
            pl.BlockSpec((None, D, D), lambda l: (l, 0, 0)),
        ],
        out_specs=[
            pl.BlockSpec((None, R, D), lambda l: (l, 0, 0)),
            pl.BlockSpec((None, R, D), lambda l: (l, 0, 0)),
        ],
        out_shape=[jax.ShapeDtypeStruct((L, R, D), BF16)] * 2,
        compiler_params=pltpu.CompilerParams(
            dimension_semantics=("arbitrary",), vmem_limit_bytes=VMEM_LIMIT),
        name="kv_proj",
    )(mem2d, mem_norm, wk, wv)


def _mixer_kernel(xc_ref, xp_ref, g1_ref, win_ref, pmap_ref, psc_ref, cw_ref, cb_ref, lg_ref, lb_ref,
                  sw_ref, wout_ref, g2_ref, o_ref,
                  h_scr, yo_scr, z0, v0, c0, y0, z1, v1, c1, y1,
                  *, d_pool, d_conf, d_sconv, tiles_per_seq):
    s = pl.program_id(0)
    n_p, n_c, n_s = d_pool // LANES, d_conf // LANES, d_sconv // LANES
    t_a, t_g = n_p, n_p + n_c
    t_b = n_p + 2 * n_c
    t_c, t_x = t_b + n_s, t_b + 2 * n_s
    d_in = win_ref.shape[-1]
    n_slices = d_in // MXU_COLS
    n_out_slices = wout_ref.shape[-1] // MXU_COLS
    assert n_slices * MIX_ROWS == MIX_TS and n_out_slices <= n_slices
    assert d_in - n_slices * MXU_COLS in (0, LANES)

    @pl.when(s == 0)
    def _init():
        for ref in (z0, v0, c0, y0, z1, v1, c1, y1):
            ref[...] = jnp.zeros(ref.shape, ref.dtype)

    def mix_rows(src, y_scr, t0, r0):
        z_scr, v_scr, c_scr = src
        lane = lax.broadcasted_iota(jnp.int32, (MIX_RC, LANES), 1)
        row = lax.broadcasted_iota(jnp.int32, (MIX_RC, LANES), 0)
        low = lane < (LANES // 2)

        for sub in range(0, MIX_ROWS, MIX_RC):
            rr = r0 + sub

            def shifted(scr, i, k, rr=rr):
                return scr[i, pl.ds(rr + (HALO - k), MIX_RC), :]

            rows = pl.ds(rr, MIX_RC)
            pos1 = t0 + rr + 1 + row
            for i in range(n_p):
                w_small, w_big = POOL_WINDOWS[2 * i], POOL_WINDOWS[2 * i + 1]
                u0 = shifted(z_scr, i, 0)
                s_small = u0
                for k in range(1, w_small):
                    s_small = s_small + shifted(z_scr, i, k)
                s_big = s_small
                for k in range(w_small, w_big):
                    s_big = s_big + shifted(z_scr, i, k)
                cnt = jnp.minimum(pos1, jnp.where(low, w_small, w_big)).astype(F32)
                pooled = jnp.where(low, s_small, s_big) / cnt - u0
                y_scr[rows, i * LANES:(i + 1) * LANES] = pooled.astype(BF16)

            conv = []
            for i in range(n_c):
                cs = slice(i * LANES, (i + 1) * LANES)
                acc = cb_ref[:, cs] + cw_ref[CONF_KERNEL - 1:CONF_KERNEL, cs] * shifted(v_scr, i, 0)
                for k in range(1, CONF_KERNEL):
                    acc = acc + cw_ref[CONF_KERNEL - 1 - k:CONF_KERNEL - k, cs] * shifted(v_scr, i, k)
                conv.append(acc)
            tot = conv[0]
            for i in range(1, n_c):
                tot = tot + conv[i]
            mu = jnp.sum(tot, axis=-1, keepdims=True) * (1.0 / d_conf)
            cen = [c - mu for c in conv]
            sq = cen[0] * cen[0]
            for i in range(1, n_c):
                sq = sq + cen[i] * cen[i]
            rstd = lax.rsqrt(jnp.sum(sq, axis=-1, keepdims=True) * (1.0 / d_conf) + EPS)
            for i in range(n_c):
                cs = slice(i * LANES, (i + 1) * LANES)
                yv = cen[i] * rstd * lg_ref[:, cs] + lb_ref[:, cs]
                y_scr[rows, d_pool + i * LANES: d_pool + (i + 1) * LANES] = (yv * _sigmoid(yv)).astype(BF16)

            for i in range(n_s):
                cs = slice(i * LANES, (i + 1) * LANES)
                acc = sw_ref[SCONV_KERNEL - 1:SCONV_KERNEL, cs] * shifted(c_scr, i, 0)
                for k in range(1, SCONV_KERNEL):
                    acc = acc + sw_ref[SCONV_KERNEL - 1 - k:SCONV_KERNEL - k, cs] * shifted(c_scr, i, k)
                gate_b = z_scr[t_b + i, pl.ds(rr + HALO, MIX_RC), :]
                off = d_pool + d_conf + i * LANES
                y_scr[rows, off:off + LANES] = (gate_b * acc).astype(BF16)

        stored = pltpu.bitcast(y_scr[pl.ds(r0, MIX_ROWS), :], F32)
        bits = lax.bitcast_convert_type(jnp.max(stored), jnp.int32)
        return lax.shift_right_logical(lax.shift_right_logical(bits, 16), 16)

    def phase(x_rows, xp_rows, out_rows, dst, src, y_mix, y_proj, tile_mix, seq_start):
        z_d, v_d, c_d = dst
        z_s, v_s, c_s = src

        def halo_from(d_ref, s_ref, n):
            tail = s_ref[0:n, MIX_TS:MIX_TS + HALO, :]
            if seq_start is not None:
                tail = jnp.where(seq_start, 0.0, tail)
            d_ref[0:n, 0:HALO, :] = tail

        h_scr[...] = _rms_scale(xc_ref[x_rows, :], g1_ref[...]).astype(BF16)
        ya = jnp.dot(y_proj[:, 0:d_pool], pmap_ref[...], preferred_element_type=F32) * psc_ref[...]
        y_proj[:, 0:d_pool] = ya.astype(BF16)
        halo_from(z_d, z_s, n_p)
        t0 = (tile_mix % tiles_per_seq) * MIX_TS

        turn = 0
        for i in range(n_slices):
            col = i * MXU_COLS + turn
            if i > 0:
                col = pl.multiple_of(col, MXU_COLS)
            z = jnp.dot(h_scr[...], win_ref[:, pl.ds(col, MXU_COLS)], preferred_element_type=F32)
            for q in range(MXU_COLS // LANES):
                z_d[(MXU_COLS // LANES) * i + q, HALO:HALO + MIX_TS, :] = z[:, q * LANES:(q + 1) * LANES]
            if i < n_out_slices:
                yo_scr[:, i * MXU_COLS:(i + 1) * MXU_COLS] = jnp.dot(
                    y_proj[...], wout_ref[:, pl.ds(col, MXU_COLS)], preferred_element_type=F32)
            turn = mix_rows(src, y_mix, t0, i * MIX_ROWS)

        if d_in > n_slices * MXU_COLS:
            col = n_slices * MXU_COLS
            z = jnp.dot(h_scr[...], win_ref[:, col:col + LANES], preferred_element_type=F32)
            z_d[col // LANES, HALO:HALO + MIX_TS, :] = z
        halo_from(v_d, v_s, n_c)
        halo_from(c_d, c_s, n_s)
        for r0 in range(0, MIX_TS, GATE_ROWS):
            body = slice(HALO + r0, HALO + r0 + GATE_ROWS)
            for i in range(n_c):
                v_d[i, body, :] = z_d[t_a + i, body, :] * _sigmoid(z_d[t_g + i, body, :])
            for i in range(n_s):
                c_d[i, body, :] = z_d[t_c + i, body, :] * z_d[t_x + i, body, :]
        o_ref[out_rows, :] = xp_ref[xp_rows, :] + _rms_scale(yo_scr[...], g2_ref[...])

    even, odd = slice(0, MIX_TS), slice(MIX_TS, 2 * MIX_TS)
    seq_start = (s % (tiles_per_seq // 2)) == 0
    set0, set1 = (z0, v0, c0), (z1, v1, c1)
    phase(even, even, even, set0, set1, y1, y0, 2 * s - 1, seq_start)
    phase(odd, odd, odd, set1, set0, y0, y1, 2 * s, None)


def _mixer_call(l, x, g1, w_in, pmap, psc, cw, cb, lg, lb, sw, w_out, g2):
    B, T, D = x.shape
    d_in = w_in.shape[-1]
    d_pool = pmap.shape[-1]
    d_conf = cw.shape[-1]
    d_sconv = sw.shape[-1]
    d_mix = w_out.shape[1]
    tiles_per_seq = T // MIX_TS
    assert tiles_per_seq % 2 == 0
    n_pairs = B * tiles_per_seq // 2
    lay = lambda s: (l, 0, 0)
    vec = lambda c: pl.BlockSpec((None, 1, c), lay)
    kern = functools.partial(_mixer_kernel, d_pool=d_pool, d_conf=d_conf, d_sconv=d_sconv,
                             tiles_per_seq=tiles_per_seq)
    staging = [
        pltpu.VMEM((d_in // LANES, HALO + MIX_TS, LANES), F32),
        pltpu.VMEM((d_conf // LANES, HALO + MIX_TS, LANES), F32),
        pltpu.VMEM((d_sconv // LANES, HALO + MIX_TS, LANES), F32),
        pltpu.VMEM((MIX_TS, d_mix), BF16),
    ]
    x2 = x.reshape(B * T, D)
    out = pl.pallas_call(
        kern,
        grid=(n_pairs + 1,),
        in_specs=[
            pl.BlockSpec((2 * MIX_TS, D), lambda s: (jnp.minimum(s, n_pairs - 1), 0)),
            pl.BlockSpec((2 * MIX_TS, D), lambda s: (jnp.maximum(s - 1, 0), 0)),
            vec(D),
            _resident((None, D, d_in), lay),
            _resident((None, d_pool, d_pool), lay),
            vec(d_pool),
            pl.BlockSpec((None, CONF_KERNEL, d_conf), lay),
            vec(d_conf), vec(d_conf), vec(d_conf),
            pl.BlockSpec((None, SCONV_KERNEL, d_sconv), lay),
            _resident((None, d_mix, D), lay),
            vec(D),
        ],
        out_specs=pl.BlockSpec((2 * MIX_TS, D), lambda s: (jnp.maximum(s - 1, 0), 0)),
        out_shape=jax.ShapeDtypeStruct((B * T, D), F32),
        scratch_shapes=[pltpu.VMEM((MIX_TS, D), BF16), pltpu.VMEM((MIX_TS, D), F32)] + staging + staging,
        compiler_params=pltpu.CompilerParams(
            dimension_semantics=("arbitrary",), vmem_limit_bytes=VMEM_LIMIT),
        name="mixer",
    )(x2, x2, g1, w_in, pmap, psc, cw, cb, lg, lb, sw, w_out, g2)
    return out.reshape(B, T, D)


XATTN_TS = 1024
FF_GROUP = 4


def _xattn_kernel(x_ref, g1_ref, wq_ref, k_ref, v_ref, wo_ref, g2_ref, wup_ref, wdn_ref,
                  o_ref, wup_o_ref, wdn_o_ref, o_scr, *, head_dim):
    wup_o_ref[...] = wup_ref[...].astype(BF16)
    wdn_o_ref[...] = wdn_ref[...].astype(BF16)
    x = x_ref[...]
    h = _rms_scale(x, g1_ref[...]).astype(BF16)
    q = (jnp.dot(h, wq_ref[...], preferred_element_type=F32) * (head_dim ** -0.5)).astype(BF16)
    for hd in range(XATTN_HEADS):
        sl = slice(hd * head_dim, (hd + 1) * head_dim)
        s = lax.dot_general(q[:, sl], k_ref[:, sl], (((1,), (1,)), ((), ())),
                            preferred_element_type=F32)
        m = jnp.max(s, axis=-1, keepdims=True)
        p = jnp.exp(s - m)
        inv_l = 1.0 / jnp.sum(p, axis=-1, keepdims=True)
        o = jnp.dot(p.astype(BF16), v_ref[:, sl], preferred_element_type=F32) * inv_l
        o_scr[:, sl] = o.astype(BF16)
    y = jnp.dot(o_scr[...], wo_ref[...], preferred_element_type=F32)
    o_ref[...] = x + _rms_scale(y, g2_ref[...])


def _xattn_call(l, x, g1, wq, k_all, v_all, wo, g2, n_mem, w_up, w_dn):
    B, T, D = x.shape
    steps = T // XATTN_TS
    up_rows, dn_rows = w_up.shape[1] // (B * steps), w_dn.shape[1] // (B * steps)
    assert up_rows * B * steps == w_up.shape[1] and dn_rows * B * steps == w_dn.shape[1]
    slab = lambda b, j: (l, b * steps + j, 0)
    slab_out = lambda b, j: (b * steps + j, 0)
    lay = lambda b, j: (l, 0, 0)
    vec = pl.BlockSpec((None, 1, D), lay)
    kern = functools.partial(_xattn_kernel, head_dim=D // XATTN_HEADS)
    return pl.pallas_call(
        kern,
        grid=(B, T // XATTN_TS),
        in_specs=[
            pl.BlockSpec((None, XATTN_TS, D), lambda b, j: (b, j, 0)),
            vec,
            _resident((None, D, D), lay),
            pl.BlockSpec((None, n_mem, D), lambda b, j: (l, b, 0)),
            pl.BlockSpec((None, n_mem, D), lambda b, j: (l, b, 0)),
            _resident((None, D, D), lay),
            vec,
            pl.BlockSpec((None, up_rows, w_up.shape[2]), slab),
            pl.BlockSpec((None, dn_rows, w_dn.shape[2]), slab),
        ],
        out_specs=[
            pl.BlockSpec((None, XATTN_TS, D), lambda b, j: (b, j, 0)),
            pl.BlockSpec((up_rows, w_up.shape[2]), slab_out),
            pl.BlockSpec((dn_rows, w_dn.shape[2]), slab_out),
        ],
        out_shape=[
            jax.ShapeDtypeStruct((B, T, D), F32),
            jax.ShapeDtypeStruct(w_up.shape[1:], BF16),
            jax.ShapeDtypeStruct(w_dn.shape[1:], BF16),
        ],
        scratch_shapes=[pltpu.VMEM((XATTN_TS, D), BF16)],
        compiler_params=pltpu.CompilerParams(
            dimension_semantics=("arbitrary", "arbitrary"), vmem_limit_bytes=VMEM_LIMIT),
        name="xattn",
    )(x, g1, wq, k_all, v_all, wo, g2, w_up, w_dn)


def _ffn_kernel(x_ref, g1_ref, wup_ref, cw_ref, wdn_ref, g2_ref, o_ref,
                h_scr, ua_scr, ub_scr, hal_scr, a_scr, *, n_chunks):
    j = pl.program_id(1)
    tiles = FF_CHUNK // LANES

    @pl.when(j == 0)
    def _zero_halo():
        hal_scr[...] = jnp.zeros(hal_scr.shape, F32)

    h_scr[...] = _rms_scale(x_ref[...], g1_ref[...]).astype(BF16)

    def up(c, u_scr):
        for half in range(2):
            wi = c + half * n_chunks
            col = wi * FF_CHUNK
            if not isinstance(col, int):
                col = pl.multiple_of(col, FF_CHUNK)
            u = jnp.dot(h_scr[...], wup_ref[:, pl.ds(col, FF_CHUNK)],
                        preferred_element_type=F32)
            for i in range(tiles):
                ct = wi * tiles + i
                s = half * tiles + i
                u_scr[s, 0:FFN_HALO, :] = hal_scr[ct]
                u_scr[s, FFN_HALO:FFN_HALO + TS, :] = u[:, i * LANES:(i + 1) * LANES]
                hal_scr[ct] = u[TS - FFN_HALO:TS, i * LANES:(i + 1) * LANES]

    def act(c, u_scr):
        for i in range(tiles):
            taps = []
            for ct in (c * tiles + i, (c + n_chunks) * tiles + i):
                w = cw_ref[ct]
                taps.append([jnp.broadcast_to(w[FFN_CONV_KERNEL - 1 - k:FFN_CONV_KERNEL - k, :], (RC, LANES))
                             for k in range(FFN_CONV_KERNEL)])
            col = c * FF_CHUNK + i * LANES
            if not isinstance(col, int):
                col = pl.multiple_of(col, LANES)
            for r0 in range(0, TS, RC):
                gv = []
                for s, w in ((i, taps[0]), (tiles + i, taps[1])):
                    base = r0 + FFN_HALO
                    acc = w[0] * u_scr[s, base:base + RC, :]
                    for k in range(1, FFN_CONV_KERNEL):
                        acc = acc + w[k] * u_scr[s, base - k:base - k + RC, :]
                    gv.append(acc)
                a = gv[0] * _sigmoid(gv[0]) * gv[1]
                a_scr[r0:r0 + RC, pl.ds(col, LANES)] = a.astype(BF16)

    assert FF_GROUP % 2 == 0
    up(0, ua_scr)

    bufs = (ua_scr, ub_scr)
    n_groups = (n_chunks - 1) // FF_GROUP

    def group(p, carry):
        c0 = FF_GROUP * p
        for q in range(FF_GROUP):
            up(c0 + q + 1, bufs[(q + 1) % 2])
            act(c0 + q, bufs[q % 2])
        return carry

    lax.fori_loop(0, n_groups, group, 0)
    for c in range(FF_GROUP * n_groups, n_chunks):
        if c + 1 < n_chunks:
            up(c + 1, bufs[(c + 1) % 2])
        act(c, bufs[c % 2])

    y = jnp.dot(a_scr[...], wdn_ref[...], preferred_element_type=F32)
    o_ref[...] = x_ref[...] + _rms_scale(y, g2_ref[...])


def _ffn_call(l, x, g1, w_up, cw, w_dn, g2):
    B, T, D = x.shape
    n_chunks = w_dn.shape[0] // FF_CHUNK
    n_ct = cw.shape[1]
    lay3 = lambda b, j: (l, 0, 0)
    lay4 = lambda b, j: (l, 0, 0, 0)
    vec = pl.BlockSpec((None, 1, D), lay3)
    kern = functools.partial(_ffn_kernel, n_chunks=n_chunks)
    return pl.pallas_call(
        kern,
        grid=(B, T // TS),
        in_specs=[
            pl.BlockSpec((None, TS, D), lambda b, j: (b, j, 0)),
            vec,
            _resident((D, 2 * n_chunks * FF_CHUNK), lambda b, j: (0, 0)),
            pl.BlockSpec((None, n_ct, FFN_CONV_KERNEL, LANES), lay4),
            _resident((n_chunks * FF_CHUNK, D), lambda b, j: (0, 0)),
            vec,
        ],
        out_specs=pl.BlockSpec((None, TS, D), lambda b, j: (b, j, 0)),
        out_shape=jax.ShapeDtypeStruct((B, T, D), F32),
        scratch_shapes=[
            pltpu.VMEM((TS, D), BF16),
            pltpu.VMEM((2 * FF_CHUNK // LANES, FFN_HALO + TS, LANES), F32),
            pltpu.VMEM((2 * FF_CHUNK // LANES, FFN_HALO + TS, LANES), F32),
            pltpu.VMEM((n_ct, FFN_HALO, LANES), F32),
            pltpu.VMEM((TS, n_chunks * FF_CHUNK), BF16),
        ],
        compiler_params=pltpu.CompilerParams(
            dimension_semantics=("arbitrary", "arbitrary"), vmem_limit_bytes=VMEM_LIMIT),
        name="ffn",
    )(x, g1, w_up, cw, w_dn, g2)


def _block_diag(maps):
    L, G, P, _ = maps.shape
    eye = jnp.eye(G, dtype=maps.dtype)
    return jnp.einsum("lgpq,gh->lgphq", maps, eye).reshape(L, G * P, G * P)


def kernel(x, mem, mem_norm, mix_pre_norm, mix_post_norm, w_in, pool_maps, pool_scale, conf_dw_w, conf_dw_b, conf_ln_g, conf_ln_b, sconv_w, w_out, xattn_pre_norm, xattn_post_norm, xattn_wq, xattn_wk, xattn_wv, xattn_wo, ffn_pre_norm, ffn_post_norm, ffn_w_up, ffn_conv_w, ffn_w_down):
    B, T, D = x.shape
    L = w_in.shape[0]
    n_mem = mem.shape[1]
    d_ff = ffn_w_down.shape[1]
    assert T % TS == 0 and T % XATTN_TS == 0 and d_ff % FF_CHUNK == 0 and D % LANES == 0

    row = lambda p: p.reshape(L, 1, p.shape[-1])
    bf = lambda w: w.astype(BF16)

    ffn_cw = ffn_conv_w.reshape(L, FFN_CONV_KERNEL, 2 * d_ff // LANES, LANES).transpose(0, 2, 1, 3)
    pmap = bf(_block_diag(pool_maps))

    k_all, v_all = _kv_call(mem.reshape(B * n_mem, D), mem_norm.reshape(1, D), xattn_wk, xattn_wv)
    w_in_b, w_out_b, wq_b, wo_b = bf(w_in), bf(w_out), bf(xattn_wq), bf(xattn_wo)

    for l in range(L):
        x = _mixer_call(l, x, row(mix_pre_norm), w_in_b, pmap, row(pool_scale), conf_dw_w,
                        row(conf_dw_b), row(conf_ln_g), row(conf_ln_b), sconv_w, w_out_b,
                        row(mix_post_norm))
        x, w_up_b, w_dn_b = _xattn_call(l, x, row(xattn_pre_norm), wq_b, k_all, v_all, wo_b,
                                        row(xattn_post_norm), n_mem, ffn_w_up, ffn_w_down)
        x = _ffn_call(l, x, row(ffn_pre_norm), w_up_b, ffn_cw, w_dn_b, row(ffn_post_norm))
    return x
```

```python
import functools

import jax
import jax.numpy as jnp
from jax import lax
from jax.experimental import pallas as pl
from jax.experimental.pallas import tpu as pltpu

F32 = jnp.float32
BF16 = jnp.bfloat16

EPS = 1e-6
LANES = 128
POOL_WINDOWS = (2, 4, 8, 16)
CONF_KERNEL = 31
SCONV_KERNEL = 3
FFN_CONV_KERNEL = 3
XATTN_HEADS = 4

TS = 512
HALO = 32
FFN_HALO = 8
RC = 64
MXU_COLS = 256
MIX_TS = 512
MIX_ROWS = 64
MIX_RC = 32
GATE_ROWS = 128
FF_CHUNK = 256
VMEM_LIMIT = 56 * 1024 * 1024


def _rms_scale(x, g):
    ms = jnp.mean(x * x, axis=-1, keepdims=True)
    return x * lax.rsqrt(ms + EPS) * g


def _sigmoid(x):
    return 1.0 / (1.0 + jnp.exp(-x))


def _resident(shape, index_map):
    return pl.BlockSpec(shape, index_map, pipeline_mode=pl.Buffered(1))


def _kv_kernel(mem_ref, g_ref, wk_ref, wv_ref, k_ref, v_ref):
    mn = _rms_scale(mem_ref[...], g_ref[...]).astype(BF16)
    k_ref[...] = jnp.dot(mn, wk_ref[...].astype(BF16), preferred_element_type=F32).astype(BF16)
    v_ref[...] = jnp.dot(mn, wv_ref[...].astype(BF16), preferred_element_type=F32).astype(BF16)


def _kv_call(mem2d, mem_norm, wk, wv):
    L, D, _ = wk.shape
    R = mem2d.shape[0]
    return pl.pallas_call(
        _kv_kernel,
        grid=(L,),
        in_specs=[
            pl.BlockSpec((R, D), lambda l: (0, 0)),
            pl.BlockSpec((1, D), lambda l: (0, 0)),
            pl.BlockSpec((None, D, D), lambda l: (l, 0, 0)),
            pl.BlockSpec((None, D, D), lambda l: (l, 0, 0)),
        ],
        out_specs=[
            pl.BlockSpec((None, R, D), lambda l: (l, 0, 0)),
            pl.BlockSpec((None, R, D), lambda l: (l, 0, 0)),
        ],
        out_shape=[jax.ShapeDtypeStruct((L, R, D), BF16)] * 2,
        compiler_params=pltpu.CompilerParams(
            dimension_semantics=("arbitrary",), vmem_limit_bytes=VMEM_LIMIT),
        name="kv_proj",
    )(mem2d, mem_norm, wk, wv)


def _mixer_kernel(xc_ref, xp_ref, g1_ref, win_ref, pmap_ref, psc_ref, cw_ref, cb_ref, lg_ref, lb_ref,
                  sw_ref, wout_ref, g2_ref, o_ref,
                  h_scr, yo_scr, z0, v0, c0, y0, z1, v1, c1, y1,
                  *, d_pool, d_conf, d_sconv, tiles_per_seq):
    s = pl.program_id(0)
    n_p, n_c, n_s = d_pool // LANES, d_conf // LANES, d_sconv // LANES
    t_a, t_g = n_p, n_p + n_c
    t_b = n_p + 2 * n_c
    t_c, t_x = t_b + n_s, t_b + 2 * n_s
    d_in = win_ref.shape[-1]
    n_slices = d_in // MXU_COLS
    n_out_slices = wout_ref.shape[-1] // MXU_COLS
    assert n_slices * MIX_ROWS == MIX_TS and n_out_slices <= n_slices
    assert d_in - n_slices * MXU_COLS in (0, LANES)

    @pl.when(s == 0)
    def _init():
        for ref in (z0, v0, c0, y0, z1, v1, c1, y1):
            ref[...] = jnp.zeros(ref.shape, ref.dtype)

    def mix_rows(src, y_scr, t0, r0):
        z_scr, v_scr, c_scr = src
        lane = lax.broadcasted_iota(jnp.int32, (MIX_RC, LANES), 1)
        row = lax.broadcasted_iota(jnp.int32, (MIX_RC, LANES), 0)
        low = lane < (LANES // 2)

        for sub in range(0, MIX_ROWS, MIX_RC):
            rr = r0 + sub

            def shifted(scr, i, k, rr=rr):
                return scr[i, pl.ds(rr + (HALO - k), MIX_RC), :]

            rows = pl.ds(rr, MIX_RC)
            pos1 = t0 + rr + 1 + row
            for i in range(n_p):
                w_small, w_big = POOL_WINDOWS[2 * i], POOL_WINDOWS[2 * i + 1]
                u0 = shifted(z_scr, i, 0)
                s_small = u0
                for k in range(1, w_small):
                    s_small = s_small + shifted(z_scr, i, k)
                s_big = s_small
                for k in range(w_small, w_big):
                    s_big = s_big + shifted(z_scr, i, k)
                cnt = jnp.minimum(pos1, jnp.where(low, w_small, w_big)).astype(F32)
                pooled = jnp.where(low, s_small, s_big) / cnt - u0
                y_scr[rows, i * LANES:(i + 1) * LANES] = pooled.astype(BF16)

            conv = []
            for i in range(n_c):
                cs = slice(i * LANES, (i + 1) * LANES)
                acc = cb_ref[:, cs] + cw_ref[CONF_KERNEL - 1:CONF_KERNEL, cs] * shifted(v_scr, i, 0)
                for k in range(1, CONF_KERNEL):
                    acc = acc + cw_ref[CONF_KERNEL - 1 - k:CONF_KERNEL - k, cs] * shifted(v_scr, i, k)
                conv.append(acc)
            tot = conv[0]
            for i in range(1, n_c):
                tot = tot + conv[i]
            mu = jnp.sum(tot, axis=-1, keepdims=True) * (1.0 / d_conf)
            cen = [c - mu for c in conv]
            sq = cen[0] * cen[0]
            for i in range(1, n_c):
                sq = sq + cen[i] * cen[i]
            rstd = lax.rsqrt(jnp.sum(sq, axis=-1, keepdims=True) * (1.0 / d_conf) + EPS)
            for i in range(n_c):
                cs = slice(i * LANES, (i + 1) * LANES)
                yv = cen[i] * rstd * lg_ref[:, cs] + lb_ref[:, cs]
                y_scr[rows, d_pool + i * LANES: d_pool + (i + 1) * LANES] = (yv * _sigmoid(yv)).astype(BF16)

            for i in range(n_s):
                cs = slice(i * LANES, (i + 1) * LANES)
                acc = sw_ref[SCONV_KERNEL - 1:SCONV_KERNEL, cs] * shifted(c_scr, i, 0)
                for k in range(1, SCONV_KERNEL):
                    acc = acc + sw_ref[SCONV_KERNEL - 1 - k:SCONV_KERNEL - k, cs] * shifted(c_scr, i, k)
                gate_b = z_scr[t_b + i, pl.ds(rr + HALO, MIX_RC), :]
                off = d_pool + d_conf + i * LANES
                y_scr[rows, off:off + LANES] = (gate_b * acc).astype(BF16)

        stored = pltpu.bitcast(y_scr[pl.ds(r0, MIX_ROWS), :], F32)
        bits = lax.bitcast_convert_type(jnp.max(stored), jnp.int32)
        return lax.shift_right_logical(lax.shift_right_logical(bits, 16), 16)

    def phase(x_rows, xp_rows, out_rows, dst, src, y_mix, y_proj, tile_mix, seq_start):
        z_d, v_d, c_d = dst
        z_s, v_s, c_s = src

        def halo_from(d_ref, s_ref, n):
            tail = s_ref[0:n, MIX_TS:MIX_TS + HALO, :]
            if seq_start is not None:
                tail = jnp.where(seq_start, 0.0, tail)
            d_ref[0:n, 0:HALO, :] = tail

        h_scr[...] = _rms_scale(xc_ref[x_rows, :], g1_ref[...]).astype(BF16)
        ya = jnp.dot(y_proj[:, 0:d_pool], pmap_ref[...], preferred_element_type=F32) * psc_ref[...]
        y_proj[:, 0:d_pool] = ya.astype(BF16)
        halo_from(z_d, z_s, n_p)
        t0 = (tile_mix % tiles_per_seq) * MIX_TS

        turn = 0
        for i in range(n_slices):
            col = i * MXU_COLS + turn
            if i > 0:
                col = pl.multiple_of(col, MXU_COLS)
            z = jnp.dot(h_scr[...], win_ref[:, pl.ds(col, MXU_COLS)], preferred_element_type=F32)
            for q in range(MXU_COLS // LANES):
                z_d[(MXU_COLS // LANES) * i + q, HALO:HALO + MIX_TS, :] = z[:, q * LANES:(q + 1) * LANES]
            if i < n_out_slices:
                yo_scr[:, i * MXU_COLS:(i + 1) * MXU_COLS] = jnp.dot(
                    y_proj[...], wout_ref[:, pl.ds(col, MXU_COLS)], preferred_element_type=F32)
            turn = mix_rows(src, y_mix, t0, i * MIX_ROWS)

        if d_in > n_slices * MXU_COLS:
            col = n_slices * MXU_COLS
            z = jnp.dot(h_scr[...], win_ref[:, col:col + LANES], preferred_element_type=F32)
            z_d[col // LANES, HALO:HALO + MIX_TS, :] = z
        halo_from(v_d, v_s, n_c)
        halo_from(c_d, c_s, n_s)
        for r0 in range(0, MIX_TS, GATE_ROWS):
            body = slice(HALO + r0, HALO + r0 + GATE_ROWS)
            for i in range(n_c):
                v_d[i, body, :] = z_d[t_a + i, body, :] * _sigmoid(z_d[t_g + i, body, :])
            for i in range(n_s):
                c_d[i, body, :] = z_d[t_c + i, body, :] * z_d[t_x + i, body, :]
        o_ref[out_rows, :] = xp_ref[xp_rows, :] + _rms_scale(yo_scr[...], g2_ref[...])

    even, odd = slice(0, MIX_TS), slice(MIX_TS, 2 * MIX_TS)
    seq_start = (s % (tiles_per_seq // 2)) == 0
    set0, set1 = (z0, v0, c0), (z1, v1, c1)
    phase(even, even, even, set0, set1, y1, y0, 2 * s - 1, seq_start)
    phase(odd, odd, odd, set1, set0, y0, y1, 2 * s, None)


def _mixer_call(l, x, g1, w_in, pmap, psc, cw, cb, lg, lb, sw, w_out, g2):
    B, T, D = x.shape
    d_in = w_in.shape[-1]
    d_pool = pmap.shape[-1]
    d_conf = cw.shape[-1]
    d_sconv = sw.shape[-1]
    d_mix = w_out.shape[1]
    tiles_per_seq = T // MIX_TS
    assert tiles_per_seq % 2 == 0
    n_pairs = B * tiles_per_seq // 2
    lay = lambda s: (l, 0, 0)
    vec = lambda c: pl.BlockSpec((None, 1, c), lay)
    kern = functools.partial(_mixer_kernel, d_pool=d_pool, d_conf=d_conf, d_sconv=d_sconv,
                             tiles_per_seq=tiles_per_seq)
    staging = [
        pltpu.VMEM((d_in // LANES, HALO + MIX_TS, LANES), F32),
        pltpu.VMEM((d_conf // LANES, HALO + MIX_TS, LANES), F32),
        pltpu.VMEM((d_sconv // LANES, HALO + MIX_TS, LANES), F32),
        pltpu.VMEM((MIX_TS, d_mix), BF16),
    ]
    x2 = x.reshape(B * T, D)
    out = pl.pallas_call(
        kern,
        grid=(n_pairs + 1,),
        in_specs=[
            pl.BlockSpec((2 * MIX_TS, D), lambda s: (jnp.minimum(s, n_pairs - 1), 0)),
            pl.BlockSpec((2 * MIX_TS, D), lambda s: (jnp.maximum(s - 1, 0), 0)),
            vec(D),
            _resident((None, D, d_in), lay),
            _resident((None, d_pool, d_pool), lay),
            vec(d_pool),
            pl.BlockSpec((None, CONF_KERNEL, d_conf), lay),
            vec(d_conf), vec(d_conf), vec(d_conf),
            pl.BlockSpec((None, SCONV_KERNEL, d_sconv), lay),
            _resident((None, d_mix, D), lay),
            vec(D),
        ],
        out_specs=pl.BlockSpec((2 * MIX_TS, D), lambda s: (jnp.maximum(s - 1, 0), 0)),
        out_shape=jax.ShapeDtypeStruct((B * T, D), F32),
        scratch_shapes=[pltpu.VMEM((MIX_TS, D), BF16), pltpu.VMEM((MIX_TS, D), F32)] + staging + staging,
        compiler_params=pltpu.CompilerParams(
            dimension_semantics=("arbitrary",), vmem_limit_bytes=VMEM_LIMIT),
        name="mixer",
    )(x2, x2, g1, w_in, pmap, psc, cw, cb, lg, lb, sw, w_out, g2)
    return out.reshape(B, T, D)


XATTN_TS = 1024
FF_GROUP = 4


def _xattn_kernel(x_ref, g1_ref, wq_ref, k_ref, v_ref, wo_ref, g2_ref, wup_ref, wdn_ref,
                  o_ref, wup_o_ref, wdn_o_ref, o_scr, *, head_dim):
    wup_o_ref[...] = wup_ref[...].astype(BF16)
    wdn_o_ref[...] = wdn_ref[...].astype(BF16)
    x = x_ref[...]
    h = _rms_scale(x, g1_ref[...]).astype(BF16)
    q = (jnp.dot(h, wq_ref[...], preferred_element_type=F32) * (head_dim ** -0.5)).astype(BF16)
    for hd in range(XATTN_HEADS):
        sl = slice(hd * head_dim, (hd + 1) * head_dim)
        s = lax.dot_general(q[:, sl], k_ref[:, sl], (((1,), (1,)), ((), ())),
                            preferred_element_type=F32)
        m = jnp.max(s, axis=-1, keepdims=True)
        p = jnp.exp(s - m)
        inv_l = 1.0 / jnp.sum(p, axis=-1, keepdims=True)
        o = jnp.dot(p.astype(BF16), v_ref[:, sl], preferred_element_type=F32) * inv_l
        o_scr[:, sl] = o.astype(BF16)
    y = jnp.dot(o_scr[...], wo_ref[...], preferred_element_type=F32)
    o_ref[...] = x + _rms_scale(y, g2_ref[...])


def _xattn_call(l, x, g1, wq, k_all, v_all, wo, g2, n_mem, w_up, w_dn):
    B, T, D = x.shape
    steps = T // XATTN_TS
    up_rows, dn_rows = w_up.shape[1] // (B * steps), w_dn.shape[1] // (B * steps)
    assert up_rows * B * steps == w_up.shape[1] and dn_rows * B * steps == w_dn.shape[1]
    slab = lambda b, j: (l, b * steps + j, 0)
    slab_out = lambda b, j: (b * steps + j, 0)
    lay = lambda b, j: (l, 0, 0)
    vec = pl.BlockSpec((None, 1, D), lay)
    kern = functools.partial(_xattn_kernel, head_dim=D // XATTN_HEADS)
    return pl.pallas_call(
        kern,
        grid=(B, T // XATTN_TS),
        in_specs=[
            pl.BlockSpec((None, XATTN_TS, D), lambda b, j: (b, j, 0)),
            vec,
            _resident((None, D, D), lay),
            pl.BlockSpec((None, n_mem, D), lambda b, j: (l, b, 0)),
            pl.BlockSpec((None, n_mem, D), lambda b, j: (l, b, 0)),
            _resident((None, D, D), lay),
            vec,
            pl.BlockSpec((None, up_rows, w_up.shape[2]), slab),
            pl.BlockSpec((None, dn_rows, w_dn.shape[2]), slab),
        ],
        out_specs=[
            pl.BlockSpec((None, XATTN_TS, D), lambda b, j: (b, j, 0)),
            pl.BlockSpec((up_rows, w_up.shape[2]), slab_out),
            pl.BlockSpec((dn_rows, w_dn.shape[2]), slab_out),
        ],
        out_shape=[
            jax.ShapeDtypeStruct((B, T, D), F32),
            jax.ShapeDtypeStruct(w_up.shape[1:], BF16),
            jax.ShapeDtypeStruct(w_dn.shape[1:], BF16),
        ],
        scratch_shapes=[pltpu.VMEM((XATTN_TS, D), BF16)],
        compiler_params=pltpu.CompilerParams(
            dimension_semantics=("arbitrary", "arbitrary"), vmem_limit_bytes=VMEM_LIMIT),
        name="xattn",
    )(x, g1, wq, k_all, v_all, wo, g2, w_up, w_dn)


def _ffn_kernel(x_ref, g1_ref, wup_ref, cw_ref, wdn_ref, g2_ref, o_ref,
                ha_scr, hb_scr, ua_scr, ub_scr, hal_scr, a_scr, *, n_chunks):
    j = pl.program_id(1)
    tiles = FF_CHUNK // LANES

    @pl.when(j == 0)
    def _zero_halo():
        hal_scr[...] = jnp.zeros(hal_scr.shape, F32)

    def up(c, u_scr, h_scr):
        for half in range(2):
            wi = c + half * n_chunks
            col = wi * FF_CHUNK
            if not isinstance(col, int):
                col = pl.multiple_of(col, FF_CHUNK)
            u = jnp.dot(h_scr[...], wup_ref[:, pl.ds(col, FF_CHUNK)],
                        preferred_element_type=F32)
            for i in range(tiles):
                ct = wi * tiles + i
                s = half * tiles + i
                u_scr[s, 0:FFN_HALO, :] = hal_scr[ct]
                u_scr[s, FFN_HALO:FFN_HALO + TS, :] = u[:, i * LANES:(i + 1) * LANES]
                hal_scr[ct] = u[TS - FFN_HALO:TS, i * LANES:(i + 1) * LANES]

    def act(c, u_scr):
        for i in range(tiles):
            taps = []
            for ct in (c * tiles + i, (c + n_chunks) * tiles + i):
                w = cw_ref[ct]
                taps.append([jnp.broadcast_to(w[FFN_CONV_KERNEL - 1 - k:FFN_CONV_KERNEL - k, :], (RC, LANES))
                             for k in range(FFN_CONV_KERNEL)])
            col = c * FF_CHUNK + i * LANES
            if not isinstance(col, int):
                col = pl.multiple_of(col, LANES)
            for r0 in range(0, TS, RC):
                gv = []
                for s, w in ((i, taps[0]), (tiles + i, taps[1])):
                    base = r0 + FFN_HALO
                    acc = w[0] * u_scr[s, base:base + RC, :]
                    for k in range(1, FFN_CONV_KERNEL):
                        acc = acc + w[k] * u_scr[s, base - k:base - k + RC, :]
                    gv.append(acc)
                a = gv[0] * _sigmoid(gv[0]) * gv[1]
                a_scr[r0:r0 + RC, pl.ds(col, LANES)] = a.astype(BF16)

    assert FF_GROUP % 2 == 0
    bufs = (ua_scr, ub_scr)
    n_groups = (n_chunks - 1) // FF_GROUP

    for rows, h_scr in ((slice(0, TS), ha_scr), (slice(TS, 2 * TS), hb_scr)):
        h_scr[...] = _rms_scale(x_ref[rows, :], g1_ref[...]).astype(BF16)

        up(0, ua_scr, h_scr)

        def group(p, carry, h_scr=h_scr):
            c0 = FF_GROUP * p
            for q in range(FF_GROUP):
                up(c0 + q + 1, bufs[(q + 1) % 2], h_scr)
                act(c0 + q, bufs[q % 2])
            return carry

        lax.fori_loop(0, n_groups, group, 0)
        for c in range(FF_GROUP * n_groups, n_chunks):
            if c + 1 < n_chunks:
                up(c + 1, bufs[(c + 1) % 2], h_scr)
            act(c, bufs[c % 2])

        y = jnp.dot(a_scr[...], wdn_ref[...], preferred_element_type=F32)
        o_ref[rows, :] = x_ref[rows, :] + _rms_scale(y, g2_ref[...])


def _ffn_call(l, x, g1, w_up, cw, w_dn, g2):
    B, T, D = x.shape
    n_chunks = w_dn.shape[0] // FF_CHUNK
    n_ct = cw.shape[1]
    lay3 = lambda b, j: (l, 0, 0)
    lay4 = lambda b, j: (l, 0, 0, 0)
    vec = pl.BlockSpec((None, 1, D), lay3)
    kern = functools.partial(_ffn_kernel, n_chunks=n_chunks)
    return pl.pallas_call(
        kern,
        grid=(B, T // (2 * TS)),
        in_specs=[
            pl.BlockSpec((None, 2 * TS, D), lambda b, j: (b, j, 0)),
            vec,
            _resident((D, 2 * n_chunks * FF_CHUNK), lambda b, j: (0, 0)),
            pl.BlockSpec((None, n_ct, FFN_CONV_KERNEL, LANES), lay4),
            _resident((n_chunks * FF_CHUNK, D), lambda b, j: (0, 0)),
            vec,
        ],
        out_specs=pl.BlockSpec((None, 2 * TS, D), lambda b, j: (b, j, 0)),
        out_shape=jax.ShapeDtypeStruct((B, T, D), F32),
        scratch_shapes=[
            pltpu.VMEM((TS, D), BF16),
            pltpu.VMEM((TS, D), BF16),
            pltpu.VMEM((2 * FF_CHUNK // LANES, FFN_HALO + TS, LANES), F32),
            pltpu.VMEM((2 * FF_CHUNK // LANES, FFN_HALO + TS, LANES), F32),
            pltpu.VMEM((n_ct, FFN_HALO, LANES), F32),
            pltpu.VMEM((TS, n_chunks * FF_CHUNK), BF16),
        ],
        compiler_params=pltpu.CompilerParams(
            dimension_semantics=("arbitrary", "arbitrary"), vmem_limit_bytes=VMEM_LIMIT),
        name="ffn",
    )(x, g1, w_up, cw, w_dn, g2)


def _block_diag(maps):
    L, G, P, _ = maps.shape
    eye = jnp.eye(G, dtype=maps.dtype)
    return jnp.einsum("lgpq,gh->lgphq", maps, eye).reshape(L, G * P, G * P)


def kernel(x, mem, mem_norm, mix_pre_norm, mix_post_norm, w_in, pool_maps, pool_scale, conf_dw_w, conf_dw_b, conf_ln_g, conf_ln_b, sconv_w, w_out, xattn_pre_norm, xattn_post_norm, xattn_wq, xattn_wk, xattn_wv, xattn_wo, ffn_pre_norm, ffn_post_norm, ffn_w_up, ffn_conv_w, ffn_w_down):
    B, T, D = x.shape
    L = w_in.shape[0]
    n_mem = mem.shape[1]
    d_ff = ffn_w_down.shape[1]
    assert T % (2 * TS) == 0 and T % XATTN_TS == 0 and d_ff % FF_CHUNK == 0 and D % LANES == 0

    row = lambda p: p.reshape(L, 1, p.shape[-1])
    bf = lambda w: w.astype(BF16)

    ffn_cw = ffn_conv_w.reshape(L, FFN_CONV_KERNEL, 2 * d_ff // LANES, LANES).transpose(0, 2, 1, 3)
    pmap = bf(_block_diag(pool_maps))

    k_all, v_all = _kv_call(mem.reshape(B * n_mem, D), mem_norm.reshape(1, D), xattn_wk, xattn_wv)
    w_in_b, w_out_b, wq_b, wo_b = bf(w_in), bf(w_out), bf(xattn_wq), bf(xattn_wo)

    for l in range(L):
        x = _mixer_call(l, x, row(mix_pre_norm), w_in_b, pmap, row(pool_scale), conf_dw_w,
                        row(conf_dw_b), row(conf_ln_g), row(conf_ln_b), sconv_w, w_out_b,
                        row(mix_post_norm))
        x, w_up_b, w_dn_b = _xattn_call(l, x, row(xattn_pre_norm), wq_b, k_all, v_all, wo_b,
                                        row(xattn_post_norm), n_mem, ffn_w_up, ffn_w_down)
        x = _ffn_call(l, x, row(ffn_pre_norm), w_up_b, ffn_cw, w_dn_b, row(ffn_post_norm))
    return x
```

```python
import functools

import jax
import jax.numpy as jnp
from jax import lax
from jax.experimental import pallas as pl
from jax.experimental.pallas import tpu as pltpu

F32 = jnp.float32
BF16 = jnp.bfloat16

EPS = 1e-6
LANES = 128
POOL_WINDOWS = (2, 4, 8, 16)
CONF_KERNEL = 31
SCONV_KERNEL = 3
FFN_CONV_KERNEL = 3
XATTN_HEADS = 4

TS = 512
HALO = 32
FFN_HALO = 8
RC = 64
MXU_COLS = 256
MIX_TS = 512
MIX_ROWS = 64
MIX_RC = 32
GATE_ROWS = 128
FF_CHUNK = 256
VMEM_LIMIT = 56 * 1024 * 1024


def _rms_scale(x, g):
    ms = jnp.mean(x * x, axis=-1, keepdims=True)
    return x * lax.rsqrt(ms + EPS) * g


def _sigmoid(x):
    return 1.0 / (1.0 + jnp.exp(-x))


def _resident(shape, index_map):
    return pl.BlockSpec(shape, index_map, pipeline_mode=pl.Buffered(1))


def _kv_kernel(mem_ref, g_ref, wk_ref, wv_ref, k_ref, v_ref):
    mn = _rms_scale(mem_ref[...], g_ref[...]).astype(BF16)
    k_ref[...] = jnp.dot(mn, wk_ref[...].astype(BF16), preferred_element_type=F32).astype(BF16)
    v_ref[...] = jnp.dot(mn, wv_ref[...].astype(BF16), preferred_element_type=F32).astype(BF16)


def _kv_call(mem2d, mem_norm, wk, wv):
    L, D, _ = wk.shape
    R = mem2d.shape[0]
    return pl.pallas_call(
        _kv_kernel,
        grid=(L,),
        in_specs=[
            pl.BlockSpec((R, D), lambda l: (0, 0)),
            pl.BlockSpec((1, D), lambda l: (0, 0)),
            pl.BlockSpec((None, D, D), lambda l: (l, 0, 0)),
            pl.BlockSpec((None, D, D), lambda l: (l, 0, 0)),
        ],
        out_specs=[
            pl.BlockSpec((None, R, D), lambda l: (l, 0, 0)),
            pl.BlockSpec((None, R, D), lambda l: (l, 0, 0)),
        ],
        out_shape=[jax.ShapeDtypeStruct((L, R, D), BF16)] * 2,
        compiler_params=pltpu.CompilerParams(
            dimension_semantics=("arbitrary",), vmem_limit_bytes=VMEM_LIMIT),
        name="kv_proj",
    )(mem2d, mem_norm, wk, wv)


def _mixer_kernel(xc_ref, xp_ref, g1_ref, win_ref, pmap_ref, psc_ref, cw_ref, cb_ref, lg_ref, lb_ref,
                  sw_ref, wout_ref, g2_ref, o_ref,
                  h_scr, yo_scr, z0, v0, c0, y0, z1, v1, c1, y1,
                  *, d_pool, d_conf, d_sconv, tiles_per_seq):
    s = pl.program_id(0)
    n_p, n_c, n_s = d_pool // LANES, d_conf // LANES, d_sconv // LANES
    t_a, t_g = n_p, n_p + n_c
    t_b = n_p + 2 * n_c
    t_c, t_x = t_b + n_s, t_b + 2 * n_s
    d_in = win_ref.shape[-1]
    n_slices = d_in // MXU_COLS
    n_out_slices = wout_ref.shape[-1] // MXU_COLS
    assert n_slices * MIX_ROWS == MIX_TS and n_out_slices <= n_slices
    assert d_in - n_slices * MXU_COLS in (0, LANES)

    @pl.when(s == 0)
    def _init():
        for ref in (z0, v0, c0, y0, z1, v1, c1, y1):
            ref[...] = jnp.zeros(ref.shape, ref.dtype)

    def mix_rows(src, y_scr, t0, r0):
        z_scr, v_scr, c_scr = src
        lane = lax.broadcasted_iota(jnp.int32, (MIX_RC, LANES), 1)
        row = lax.broadcasted_iota(jnp.int32, (MIX_RC, LANES), 0)
        low = lane < (LANES // 2)

        for sub in range(0, MIX_ROWS, MIX_RC):
            rr = r0 + sub

            def shifted(scr, i, k, rr=rr):
                return scr[i, pl.ds(rr + (HALO - k), MIX_RC), :]

            rows = pl.ds(rr, MIX_RC)
            pos1 = t0 + rr + 1 + row
            for i in range(n_p):
                w_small, w_big = POOL_WINDOWS[2 * i], POOL_WINDOWS[2 * i + 1]
                u0 = shifted(z_scr, i, 0)
                s_small = u0
                for k in range(1, w_small):
                    s_small = s_small + shifted(z_scr, i, k)
                s_big = s_small
                for k in range(w_small, w_big):
                    s_big = s_big + shifted(z_scr, i, k)
                cnt = jnp.minimum(pos1, jnp.where(low, w_small, w_big)).astype(F32)
                pooled = jnp.where(low, s_small, s_big) / cnt - u0
                y_scr[rows, i * LANES:(i + 1) * LANES] = pooled.astype(BF16)

            conv = []
            for i in range(n_c):
                cs = slice(i * LANES, (i + 1) * LANES)
                acc = cb_ref[:, cs] + cw_ref[CONF_KERNEL - 1:CONF_KERNEL, cs] * shifted(v_scr, i, 0)
                for k in range(1, CONF_KERNEL):
                    acc = acc + cw_ref[CONF_KERNEL - 1 - k:CONF_KERNEL - k, cs] * shifted(v_scr, i, k)
                conv.append(acc)
            tot = conv[0]
            for i in range(1, n_c):
                tot = tot + conv[i]
            mu = jnp.sum(tot, axis=-1, keepdims=True) * (1.0 / d_conf)
            cen = [c - mu for c in conv]
            sq = cen[0] * cen[0]
            for i in range(1, n_c):
                sq = sq + cen[i] * cen[i]
            rstd = lax.rsqrt(jnp.sum(sq, axis=-1, keepdims=True) * (1.0 / d_conf) + EPS)
            for i in range(n_c):
                cs = slice(i * LANES, (i + 1) * LANES)
                yv = cen[i] * rstd * lg_ref[:, cs] + lb_ref[:, cs]
                y_scr[rows, d_pool + i * LANES: d_pool + (i + 1) * LANES] = (yv * _sigmoid(yv)).astype(BF16)

            for i in range(n_s):
                cs = slice(i * LANES, (i + 1) * LANES)
                acc = sw_ref[SCONV_KERNEL - 1:SCONV_KERNEL, cs] * shifted(c_scr, i, 0)
                for k in range(1, SCONV_KERNEL):
                    acc = acc + sw_ref[SCONV_KERNEL - 1 - k:SCONV_KERNEL - k, cs] * shifted(c_scr, i, k)
                gate_b = z_scr[t_b + i, pl.ds(rr + HALO, MIX_RC), :]
                off = d_pool + d_conf + i * LANES
                y_scr[rows, off:off + LANES] = (gate_b * acc).astype(BF16)

        stored = pltpu.bitcast(y_scr[pl.ds(r0, MIX_ROWS), :], F32)
        bits = lax.bitcast_convert_type(jnp.max(stored), jnp.int32)
        return lax.shift_right_logical(lax.shift_right_logical(bits, 16), 16)

    def phase(x_rows, xp_rows, out_rows, dst, src, y_mix, y_proj, tile_mix, seq_start):
        z_d, v_d, c_d = dst
        z_s, v_s, c_s = src

        def halo_from(d_ref, s_ref, n):
            tail = s_ref[0:n, MIX_TS:MIX_TS + HALO, :]
            if seq_start is not None:
                tail = jnp.where(seq_start, 0.0, tail)
            d_ref[0:n, 0:HALO, :] = tail

        h_scr[...] = _rms_scale(xc_ref[x_rows, :], g1_ref[...]).astype(BF16)
        ya = jnp.dot(y_proj[:, 0:d_pool], pmap_ref[...], preferred_element_type=F32) * psc_ref[...]
        y_proj[:, 0:d_pool] = ya.astype(BF16)
        halo_from(z_d, z_s, n_p)
        t0 = (tile_mix % tiles_per_seq) * MIX_TS

        turn = 0
        for i in range(n_slices):
            col = i * MXU_COLS + turn
            if i > 0:
                col = pl.multiple_of(col, MXU_COLS)
            z = jnp.dot(h_scr[...], win_ref[:, pl.ds(col, MXU_COLS)], preferred_element_type=F32)
            for q in range(MXU_COLS // LANES):
                z_d[(MXU_COLS // LANES) * i + q, HALO:HALO + MIX_TS, :] = z[:, q * LANES:(q + 1) * LANES]
            if i < n_out_slices:
                yo_scr[:, i * MXU_COLS:(i + 1) * MXU_COLS] = jnp.dot(
                    y_proj[...], wout_ref[:, pl.ds(col, MXU_COLS)], preferred_element_type=F32)
            turn = mix_rows(src, y_mix, t0, i * MIX_ROWS)

        if d_in > n_slices * MXU_COLS:
            col = n_slices * MXU_COLS
            z = jnp.dot(h_scr[...], win_ref[:, col:col + LANES], preferred_element_type=F32)
            z_d[col // LANES, HALO:HALO + MIX_TS, :] = z
        halo_from(v_d, v_s, n_c)
        halo_from(c_d, c_s, n_s)
        for r0 in range(0, MIX_TS, GATE_ROWS):
            body = slice(HALO + r0, HALO + r0 + GATE_ROWS)
            for i in range(n_c):
                v_d[i, body, :] = z_d[t_a + i, body, :] * _sigmoid(z_d[t_g + i, body, :])
            for i in range(n_s):
                c_d[i, body, :] = z_d[t_c + i, body, :] * z_d[t_x + i, body, :]
        o_ref[out_rows, :] = xp_ref[xp_rows, :] + _rms_scale(yo_scr[...], g2_ref[...])

    even, odd = slice(0, MIX_TS), slice(MIX_TS, 2 * MIX_TS)
    seq_start = (s % (tiles_per_seq // 2)) == 0
    set0, set1 = (z0, v0, c0), (z1, v1, c1)
    phase(even, even, even, set0, set1, y1, y0, 2 * s - 1, seq_start)
    phase(odd, odd, odd, set1, set0, y0, y1, 2 * s, None)


def _mixer_call(l, x, g1, w_in, pmap, psc, cw, cb, lg, lb, sw, w_out, g2):
    B, T, D = x.shape
    d_in = w_in.shape[-1]
    d_pool = pmap.shape[-1]
    d_conf = cw.shape[-1]
    d_sconv = sw.shape[-1]
    d_mix = w_out.shape[0]
    tiles_per_seq = T // MIX_TS
    assert tiles_per_seq % 2 == 0
    n_pairs = B * tiles_per_seq // 2
    lay = lambda s: (l, 0, 0)
    vec = lambda c: pl.BlockSpec((None, 1, c), lay)
    kern = functools.partial(_mixer_kernel, d_pool=d_pool, d_conf=d_conf, d_sconv=d_sconv,
                             tiles_per_seq=tiles_per_seq)
    staging = [
        pltpu.VMEM((d_in // LANES, HALO + MIX_TS, LANES), F32),
        pltpu.VMEM((d_conf // LANES, HALO + MIX_TS, LANES), F32),
        pltpu.VMEM((d_sconv // LANES, HALO + MIX_TS, LANES), F32),
        pltpu.VMEM((MIX_TS, d_mix), BF16),
    ]
    x2 = x.reshape(B * T, D)
    out = pl.pallas_call(
        kern,
        grid=(n_pairs + 1,),
        in_specs=[
            pl.BlockSpec((2 * MIX_TS, D), lambda s: (jnp.minimum(s, n_pairs - 1), 0)),
            pl.BlockSpec((2 * MIX_TS, D), lambda s: (jnp.maximum(s - 1, 0), 0)),
            vec(D),
            _resident((D, d_in), lambda s: (0, 0)),
            _resident((None, d_pool, d_pool), lay),
            vec(d_pool),
            pl.BlockSpec((None, CONF_KERNEL, d_conf), lay),
            vec(d_conf), vec(d_conf), vec(d_conf),
            pl.BlockSpec((None, SCONV_KERNEL, d_sconv), lay),
            _resident((d_mix, D), lambda s: (0, 0)),
            vec(D),
        ],
        out_specs=pl.BlockSpec((2 * MIX_TS, D), lambda s: (jnp.maximum(s - 1, 0), 0)),
        out_shape=jax.ShapeDtypeStruct((B * T, D), F32),
        scratch_shapes=[pltpu.VMEM((MIX_TS, D), BF16), pltpu.VMEM((MIX_TS, D), F32)] + staging + staging,
        compiler_params=pltpu.CompilerParams(
            dimension_semantics=("arbitrary",), vmem_limit_bytes=VMEM_LIMIT),
        name="mixer",
    )(x2, x2, g1, w_in, pmap, psc, cw, cb, lg, lb, sw, w_out, g2)
    return out.reshape(B, T, D)


XATTN_TS = 1024
FF_GROUP = 4


def _xattn_kernel(x_ref, g1_ref, wq_ref, k_ref, v_ref, wo_ref, g2_ref, wup_ref, wdn_ref,
                  o_ref, wup_o_ref, wdn_o_ref, o_scr, *, head_dim):
    wup_o_ref[...] = wup_ref[...].astype(BF16)
    wdn_o_ref[...] = wdn_ref[...].astype(BF16)
    x = x_ref[...]
    h = _rms_scale(x, g1_ref[...]).astype(BF16)
    q = (jnp.dot(h, wq_ref[...], preferred_element_type=F32) * (head_dim ** -0.5)).astype(BF16)
    for hd in range(XATTN_HEADS):
        sl = slice(hd * head_dim, (hd + 1) * head_dim)
        s = lax.dot_general(q[:, sl], k_ref[:, sl], (((1,), (1,)), ((), ())),
                            preferred_element_type=F32)
        m = jnp.max(s, axis=-1, keepdims=True)
        p = jnp.exp(s - m)
        inv_l = 1.0 / jnp.sum(p, axis=-1, keepdims=True)
        o = jnp.dot(p.astype(BF16), v_ref[:, sl], preferred_element_type=F32) * inv_l
        o_scr[:, sl] = o.astype(BF16)
    y = jnp.dot(o_scr[...], wo_ref[...], preferred_element_type=F32)
    o_ref[...] = x + _rms_scale(y, g2_ref[...])


def _xattn_call(l, x, g1, wq, k_all, v_all, wo, g2, n_mem, w_up, w_dn):
    B, T, D = x.shape
    steps = T // XATTN_TS
    up_rows, dn_rows = w_up.shape[1] // (B * steps), w_dn.shape[1] // (B * steps)
    assert up_rows * B * steps == w_up.shape[1] and dn_rows * B * steps == w_dn.shape[1]
    slab = lambda b, j: (l, b * steps + j, 0)
    slab_out = lambda b, j: (b * steps + j, 0)
    lay = lambda b, j: (l, 0, 0)
    vec = pl.BlockSpec((None, 1, D), lay)
    kern = functools.partial(_xattn_kernel, head_dim=D // XATTN_HEADS)
    return pl.pallas_call(
        kern,
        grid=(B, T // XATTN_TS),
        in_specs=[
            pl.BlockSpec((None, XATTN_TS, D), lambda b, j: (b, j, 0)),
            vec,
            _resident((D, D), lambda b, j: (0, 0)),
            pl.BlockSpec((None, n_mem, D), lambda b, j: (l, b, 0)),
            pl.BlockSpec((None, n_mem, D), lambda b, j: (l, b, 0)),
            _resident((D, D), lambda b, j: (0, 0)),
            vec,
            pl.BlockSpec((None, up_rows, w_up.shape[2]), slab),
            pl.BlockSpec((None, dn_rows, w_dn.shape[2]), slab),
        ],
        out_specs=[
            pl.BlockSpec((None, XATTN_TS, D), lambda b, j: (b, j, 0)),
            pl.BlockSpec((up_rows, w_up.shape[2]), slab_out),
            pl.BlockSpec((dn_rows, w_dn.shape[2]), slab_out),
        ],
        out_shape=[
            jax.ShapeDtypeStruct((B, T, D), F32),
            jax.ShapeDtypeStruct(w_up.shape[1:], BF16),
            jax.ShapeDtypeStruct(w_dn.shape[1:], BF16),
        ],
        scratch_shapes=[pltpu.VMEM((XATTN_TS, D), BF16)],
        compiler_params=pltpu.CompilerParams(
            dimension_semantics=("arbitrary", "arbitrary"), vmem_limit_bytes=VMEM_LIMIT),
        name="xattn",
    )(x, g1, wq, k_all, v_all, wo, g2, w_up, w_dn)


def _ffn_kernel(x_ref, g1_ref, wup_ref, cw_ref, wdn_ref, g2_ref, *rest, n_chunks, n_cast):
    cast_in, o_ref, cast_out = rest[:n_cast], rest[n_cast], rest[n_cast + 1:2 * n_cast + 1]
    ha_scr, hb_scr, ua_scr, ub_scr, hal_scr, a_scr = rest[2 * n_cast + 1:]
    for src_ref, dst_ref in zip(cast_in, cast_out):
        dst_ref[...] = src_ref[...].astype(BF16)
    j = pl.program_id(1)
    tiles = FF_CHUNK // LANES

    @pl.when(j == 0)
    def _zero_halo():
        hal_scr[...] = jnp.zeros(hal_scr.shape, F32)

    def up(c, u_scr, h_scr):
        for half in range(2):
            wi = c + half * n_chunks
            col = wi * FF_CHUNK
            if not isinstance(col, int):
                col = pl.multiple_of(col, FF_CHUNK)
            u = jnp.dot(h_scr[...], wup_ref[:, pl.ds(col, FF_CHUNK)],
                        preferred_element_type=F32)
            for i in range(tiles):
                ct = wi * tiles + i
                s = half * tiles + i
                u_scr[s, 0:FFN_HALO, :] = hal_scr[ct]
                u_scr[s, FFN_HALO:FFN_HALO + TS, :] = u[:, i * LANES:(i + 1) * LANES]
                hal_scr[ct] = u[TS - FFN_HALO:TS, i * LANES:(i + 1) * LANES]

    def act(c, u_scr):
        for i in range(tiles):
            taps = []
            for ct in (c * tiles + i, (c + n_chunks) * tiles + i):
                w = cw_ref[ct]
                taps.append([jnp.broadcast_to(w[FFN_CONV_KERNEL - 1 - k:FFN_CONV_KERNEL - k, :], (RC, LANES))
                             for k in range(FFN_CONV_KERNEL)])
            col = c * FF_CHUNK + i * LANES
            if not isinstance(col, int):
                col = pl.multiple_of(col, LANES)
            for r0 in range(0, TS, RC):
                gv = []
                for s, w in ((i, taps[0]), (tiles + i, taps[1])):
                    base = r0 + FFN_HALO
                    acc = w[0] * u_scr[s, base:base + RC, :]
                    for k in range(1, FFN_CONV_KERNEL):
                        acc = acc + w[k] * u_scr[s, base - k:base - k + RC, :]
                    gv.append(acc)
                a = gv[0] * _sigmoid(gv[0]) * gv[1]
                a_scr[r0:r0 + RC, pl.ds(col, LANES)] = a.astype(BF16)

    assert FF_GROUP % 2 == 0
    bufs = (ua_scr, ub_scr)
    n_groups = (n_chunks - 1) // FF_GROUP

    for rows, h_scr in ((slice(0, TS), ha_scr), (slice(TS, 2 * TS), hb_scr)):
        h_scr[...] = _rms_scale(x_ref[rows, :], g1_ref[...]).astype(BF16)

        up(0, ua_scr, h_scr)

        def group(p, carry, h_scr=h_scr):
            c0 = FF_GROUP * p
            for q in range(FF_GROUP):
                up(c0 + q + 1, bufs[(q + 1) % 2], h_scr)
                act(c0 + q, bufs[q % 2])
            return carry

        lax.fori_loop(0, n_groups, group, 0)
        for c in range(FF_GROUP * n_groups, n_chunks):
            if c + 1 < n_chunks:
                up(c + 1, bufs[(c + 1) % 2], h_scr)
            act(c, bufs[c % 2])

        y = jnp.dot(a_scr[...], wdn_ref[...], preferred_element_type=F32)
        o_ref[rows, :] = x_ref[rows, :] + _rms_scale(y, g2_ref[...])


def _ffn_call(l, x, g1, w_up, cw, w_dn, g2, next_weights=()):
    B, T, D = x.shape
    steps = T // (2 * TS)
    slab_rows = [w.shape[1] // (B * steps) for w in next_weights]
    assert all(r * B * steps == w.shape[1] and r % 16 == 0 for r, w in zip(slab_rows, next_weights))
    n_chunks = w_dn.shape[0] // FF_CHUNK
    n_ct = cw.shape[1]
    lay3 = lambda b, j: (l, 0, 0)
    lay4 = lambda b, j: (l, 0, 0, 0)
    vec = pl.BlockSpec((None, 1, D), lay3)
    kern = functools.partial(_ffn_kernel, n_chunks=n_chunks, n_cast=len(next_weights))
    return pl.pallas_call(
        kern,
        grid=(B, T // (2 * TS)),
        in_specs=[
            pl.BlockSpec((None, 2 * TS, D), lambda b, j: (b, j, 0)),
            vec,
            _resident((D, 2 * n_chunks * FF_CHUNK), lambda b, j: (0, 0)),
            pl.BlockSpec((None, n_ct, FFN_CONV_KERNEL, LANES), lay4),
            _resident((n_chunks * FF_CHUNK, D), lambda b, j: (0, 0)),
            vec,
        ] + [pl.BlockSpec((None, r, w.shape[2]), lambda b, j: (l + 1, b * steps + j, 0))
             for r, w in zip(slab_rows, next_weights)],
        out_specs=[pl.BlockSpec((None, 2 * TS, D), lambda b, j: (b, j, 0))]
        + [pl.BlockSpec((r, w.shape[2]), lambda b, j: (b * steps + j, 0)) for r, w in zip(slab_rows, next_weights)],
        out_shape=[jax.ShapeDtypeStruct((B, T, D), F32)]
        + [jax.ShapeDtypeStruct(w.shape[1:], BF16) for w in next_weights],
        scratch_shapes=[
            pltpu.VMEM((TS, D), BF16),
            pltpu.VMEM((TS, D), BF16),
            pltpu.VMEM((2 * FF_CHUNK // LANES, FFN_HALO + TS, LANES), F32),
            pltpu.VMEM((2 * FF_CHUNK // LANES, FFN_HALO + TS, LANES), F32),
            pltpu.VMEM((n_ct, FFN_HALO, LANES), F32),
            pltpu.VMEM((TS, n_chunks * FF_CHUNK), BF16),
        ],
        compiler_params=pltpu.CompilerParams(
            dimension_semantics=("arbitrary", "arbitrary"), vmem_limit_bytes=VMEM_LIMIT),
        name="ffn",
    )(x, g1, w_up, cw, w_dn, g2, *next_weights)


def _block_diag(maps):
    L, G, P, _ = maps.shape
    eye = jnp.eye(G, dtype=maps.dtype)
    return jnp.einsum("lgpq,gh->lgphq", maps, eye).reshape(L, G * P, G * P)


def kernel(x, mem, mem_norm, mix_pre_norm, mix_post_norm, w_in, pool_maps, pool_scale, conf_dw_w, conf_dw_b, conf_ln_g, conf_ln_b, sconv_w, w_out, xattn_pre_norm, xattn_post_norm, xattn_wq, xattn_wk, xattn_wv, xattn_wo, ffn_pre_norm, ffn_post_norm, ffn_w_up, ffn_conv_w, ffn_w_down):
    B, T, D = x.shape
    L = w_in.shape[0]
    n_mem = mem.shape[1]
    d_ff = ffn_w_down.shape[1]
    assert T % (2 * TS) == 0 and T % XATTN_TS == 0 and d_ff % FF_CHUNK == 0 and D % LANES == 0

    row = lambda p: p.reshape(L, 1, p.shape[-1])
    bf = lambda w: w.astype(BF16)

    ffn_cw = ffn_conv_w.reshape(L, FFN_CONV_KERNEL, 2 * d_ff // LANES, LANES).transpose(0, 2, 1, 3)
    pmap = bf(_block_diag(pool_maps))

    k_all, v_all = _kv_call(mem.reshape(B * n_mem, D), mem_norm.reshape(1, D), xattn_wk, xattn_wv)
    per_layer = (w_in, w_out, xattn_wq, xattn_wo)
    w_in_b, w_out_b, wq_b, wo_b = (bf(w[0]) for w in per_layer)

    for l in range(L):
        x = _mixer_call(l, x, row(mix_pre_norm), w_in_b, pmap, row(pool_scale), conf_dw_w,
                        row(conf_dw_b), row(conf_ln_g), row(conf_ln_b), sconv_w, w_out_b,
                        row(mix_post_norm))
        x, w_up_b, w_dn_b = _xattn_call(l, x, row(xattn_pre_norm), wq_b, k_all, v_all, wo_b,
                                        row(xattn_post_norm), n_mem, ffn_w_up, ffn_w_down)
        nxt = per_layer if l + 1 < L else ()
        x, *cast = _ffn_call(l, x, row(ffn_pre_norm), w_up_b, ffn_cw, w_dn_b, row(ffn_post_norm), nxt)
        if cast:
            w_in_b, w_out_b, wq_b, wo_b = cast
    return x
```

```python
import functools

import jax
import jax.numpy as jnp
from jax import lax
from jax.experimental import pallas as pl
from jax.experimental.pallas import tpu as pltpu

F32 = jnp.float32
BF16 = jnp.bfloat16

EPS = 1e-6
LANES = 128
POOL_WINDOWS = (2, 4, 8, 16)
CONF_KERNEL = 31
SCONV_KERNEL = 3
FFN_CONV_KERNEL = 3
XATTN_HEADS = 4

TS = 512
HALO = 32
FFN_HALO = 8
RC = 64
MXU_COLS = 256
MIX_TS = 512
MIX_ROWS = 64
MIX_RC = 32
GATE_ROWS = 128
FF_CHUNK = 256
VMEM_LIMIT = 56 * 1024 * 1024


def _rms_scale(x, g):
    ms = jnp.mean(x * x, axis=-1, keepdims=True)
    return x * lax.rsqrt(ms + EPS) * g


def _sigmoid(x):
    return 1.0 / (1.0 + jnp.exp(-x))


def _resident(shape, index_map):
    return pl.BlockSpec(shape, index_map, pipeline_mode=pl.Buffered(1))


def _kv_kernel(mem_ref, g_ref, wk_ref, wv_ref, *rest, n_cast):
    cast_in, (k_ref, v_ref), cast_out = rest[:n_cast], rest[n_cast:n_cast + 2], rest[n_cast + 2:]
    for src_ref, dst_ref in zip(cast_in, cast_out):
        dst_ref[...] = src_ref[...].astype(BF16)
    mn = _rms_scale(mem_ref[...], g_ref[...]).astype(BF16)
    k_ref[...] = jnp.dot(mn, wk_ref[...].astype(BF16), preferred_element_type=F32).astype(BF16)
    v_ref[...] = jnp.dot(mn, wv_ref[...].astype(BF16), preferred_element_type=F32).astype(BF16)


def _kv_call(mem2d, mem_norm, wk, wv, first_weights):
    L, D, _ = wk.shape
    R = mem2d.shape[0]
    slab_rows = [w.shape[1] // L for w in first_weights]
    assert all(r * L == w.shape[1] and r % 16 == 0 for r, w in zip(slab_rows, first_weights))
    return pl.pallas_call(
        functools.partial(_kv_kernel, n_cast=len(first_weights)),
        grid=(L,),
        in_specs=[
            pl.BlockSpec((R, D), lambda l: (0, 0)),
            pl.BlockSpec((1, D), lambda l: (0, 0)),
            pl.BlockSpec((None, D, D), lambda l: (l, 0, 0)),
            pl.BlockSpec((None, D, D), lambda l: (l, 0, 0)),
        ] + [pl.BlockSpec((None, r, w.shape[2]), lambda l: (0, l, 0)) for r, w in zip(slab_rows, first_weights)],
        out_specs=[
            pl.BlockSpec((None, R, D), lambda l: (l, 0, 0)),
            pl.BlockSpec((None, R, D), lambda l: (l, 0, 0)),
        ] + [pl.BlockSpec((r, w.shape[2]), lambda l: (l, 0)) for r, w in zip(slab_rows, first_weights)],
        out_shape=[jax.ShapeDtypeStruct((L, R, D), BF16)] * 2
        + [jax.ShapeDtypeStruct(w.shape[1:], BF16) for w in first_weights],
        compiler_params=pltpu.CompilerParams(
            dimension_semantics=("arbitrary",), vmem_limit_bytes=VMEM_LIMIT),
        name="kv_proj",
    )(mem2d, mem_norm, wk, wv, *first_weights)


def _mixer_kernel(xc_ref, xp_ref, g1_ref, win_ref, pmap_ref, psc_ref, cw_ref, cb_ref, lg_ref, lb_ref,
                  sw_ref, wout_ref, g2_ref, o_ref,
                  h_scr, yo_scr, z0, v0, c0, y0, z1, v1, c1, y1,
                  *, d_pool, d_conf, d_sconv, tiles_per_seq):
    s = pl.program_id(0)
    n_p, n_c, n_s = d_pool // LANES, d_conf // LANES, d_sconv // LANES
    t_a, t_g = n_p, n_p + n_c
    t_b = n_p + 2 * n_c
    t_c, t_x = t_b + n_s, t_b + 2 * n_s
    d_in = win_ref.shape[-1]
    n_slices = d_in // MXU_COLS
    n_out_slices = wout_ref.shape[-1] // MXU_COLS
    assert n_slices * MIX_ROWS == MIX_TS and n_out_slices <= n_slices
    assert d_in - n_slices * MXU_COLS in (0, LANES)

    @pl.when(s == 0)
    def _init():
        for ref in (z0, v0, c0, y0, z1, v1, c1, y1):
            ref[...] = jnp.zeros(ref.shape, ref.dtype)

    def mix_rows(src, y_scr, t0, r0):
        z_scr, v_scr, c_scr = src
        lane = lax.broadcasted_iota(jnp.int32, (MIX_RC, LANES), 1)
        row = lax.broadcasted_iota(jnp.int32, (MIX_RC, LANES), 0)
        low = lane < (LANES // 2)

        for sub in range(0, MIX_ROWS, MIX_RC):
            rr = r0 + sub

            def shifted(scr, i, k, rr=rr):
                return scr[i, pl.ds(rr + (HALO - k), MIX_RC), :]

            rows = pl.ds(rr, MIX_RC)
            pos1 = t0 + rr + 1 + row
            for i in range(n_p):
                w_small, w_big = POOL_WINDOWS[2 * i], POOL_WINDOWS[2 * i + 1]
                u0 = shifted(z_scr, i, 0)
                s_small = u0
                for k in range(1, w_small):
                    s_small = s_small + shifted(z_scr, i, k)
                s_big = s_small
                for k in range(w_small, w_big):
                    s_big = s_big + shifted(z_scr, i, k)
                cnt = jnp.minimum(pos1, jnp.where(low, w_small, w_big)).astype(F32)
                pooled = jnp.where(low, s_small, s_big) / cnt - u0
                y_scr[rows, i * LANES:(i + 1) * LANES] = pooled.astype(BF16)

            conv = []
            for i in range(n_c):
                cs = slice(i * LANES, (i + 1) * LANES)
                acc = cb_ref[:, cs] + cw_ref[CONF_KERNEL - 1:CONF_KERNEL, cs] * shifted(v_scr, i, 0)
                for k in range(1, CONF_KERNEL):
                    acc = acc + cw_ref[CONF_KERNEL - 1 - k:CONF_KERNEL - k, cs] * shifted(v_scr, i, k)
                conv.append(acc)
            tot = conv[0]
            for i in range(1, n_c):
                tot = tot + conv[i]
            mu = jnp.sum(tot, axis=-1, keepdims=True) * (1.0 / d_conf)
            cen = [c - mu for c in conv]
            sq = cen[0] * cen[0]
            for i in range(1, n_c):
                sq = sq + cen[i] * cen[i]
            rstd = lax.rsqrt(jnp.sum(sq, axis=-1, keepdims=True) * (1.0 / d_conf) + EPS)
            for i in range(n_c):
                cs = slice(i * LANES, (i + 1) * LANES)
                yv = cen[i] * rstd * lg_ref[:, cs] + lb_ref[:, cs]
                y_scr[rows, d_pool + i * LANES: d_pool + (i + 1) * LANES] = (yv * _sigmoid(yv)).astype(BF16)

            for i in range(n_s):
                cs = slice(i * LANES, (i + 1) * LANES)
                acc = sw_ref[SCONV_KERNEL - 1:SCONV_KERNEL, cs] * shifted(c_scr, i, 0)
                for k in range(1, SCONV_KERNEL):
                    acc = acc + sw_ref[SCONV_KERNEL - 1 - k:SCONV_KERNEL - k, cs] * shifted(c_scr, i, k)
                gate_b = z_scr[t_b + i, pl.ds(rr + HALO, MIX_RC), :]
                off = d_pool + d_conf + i * LANES
                y_scr[rows, off:off + LANES] = (gate_b * acc).astype(BF16)

        stored = pltpu.bitcast(y_scr[pl.ds(r0, MIX_ROWS), :], F32)
        bits = lax.bitcast_convert_type(jnp.max(stored), jnp.int32)
        return lax.shift_right_logical(lax.shift_right_logical(bits, 16), 16)

    def phase(x_rows, xp_rows, out_rows, dst, src, y_mix, y_proj, tile_mix, seq_start):
        z_d, v_d, c_d = dst
        z_s, v_s, c_s = src

        def halo_from(d_ref, s_ref, n):
            tail = s_ref[0:n, MIX_TS:MIX_TS + HALO, :]
            if seq_start is not None:
                tail = jnp.where(seq_start, 0.0, tail)
            d_ref[0:n, 0:HALO, :] = tail

        h_scr[...] = _rms_scale(xc_ref[x_rows, :], g1_ref[...]).astype(BF16)
        ya = jnp.dot(y_proj[:, 0:d_pool], pmap_ref[...], preferred_element_type=F32) * psc_ref[...]
        y_proj[:, 0:d_pool] = ya.astype(BF16)
        halo_from(z_d, z_s, n_p)
        t0 = (tile_mix % tiles_per_seq) * MIX_TS

        turn = 0
        for i in range(n_slices):
            col = i * MXU_COLS + turn
            if i > 0:
                col = pl.multiple_of(col, MXU_COLS)
            z = jnp.dot(h_scr[...], win_ref[:, pl.ds(col, MXU_COLS)], preferred_element_type=F32)
            for q in range(MXU_COLS // LANES):
                z_d[(MXU_COLS // LANES) * i + q, HALO:HALO + MIX_TS, :] = z[:, q * LANES:(q + 1) * LANES]
            if i < n_out_slices:
                yo_scr[:, i * MXU_COLS:(i + 1) * MXU_COLS] = jnp.dot(
                    y_proj[...], wout_ref[:, pl.ds(col, MXU_COLS)], preferred_element_type=F32)
            turn = mix_rows(src, y_mix, t0, i * MIX_ROWS)

        if d_in > n_slices * MXU_COLS:
            col = n_slices * MXU_COLS
            z = jnp.dot(h_scr[...], win_ref[:, col:col + LANES], preferred_element_type=F32)
            z_d[col // LANES, HALO:HALO + MIX_TS, :] = z
        halo_from(v_d, v_s, n_c)
        halo_from(c_d, c_s, n_s)
        for r0 in range(0, MIX_TS, GATE_ROWS):
            body = slice(HALO + r0, HALO + r0 + GATE_ROWS)
            for i in range(n_c):
                v_d[i, body, :] = z_d[t_a + i, body, :] * _sigmoid(z_d[t_g + i, body, :])
            for i in range(n_s):
                c_d[i, body, :] = z_d[t_c + i, body, :] * z_d[t_x + i, body, :]
        o_ref[out_rows, :] = xp_ref[xp_rows, :] + _rms_scale(yo_scr[...], g2_ref[...])

    even, odd = slice(0, MIX_TS), slice(MIX_TS, 2 * MIX_TS)
    seq_start = (s % (tiles_per_seq // 2)) == 0
    set0, set1 = (z0, v0, c0), (z1, v1, c1)
    phase(even, even, even, set0, set1, y1, y0, 2 * s - 1, seq_start)
    phase(odd, odd, odd, set1, set0, y0, y1, 2 * s, None)


def _mixer_call(l, x, g1, w_in, pmap, psc, cw, cb, lg, lb, sw, w_out, g2):
    B, T, D = x.shape
    d_in = w_in.shape[-1]
    d_pool = pmap.shape[-1]
    d_conf = cw.shape[-1]
    d_sconv = sw.shape[-1]
    d_mix = w_out.shape[0]
    tiles_per_seq = T // MIX_TS
    assert tiles_per_seq % 2 == 0
    n_pairs = B * tiles_per_seq // 2
    lay = lambda s: (l, 0, 0)
    vec = lambda c: pl.BlockSpec((None, 1, c), lay)
    kern = functools.partial(_mixer_kernel, d_pool=d_pool, d_conf=d_conf, d_sconv=d_sconv,
                             tiles_per_seq=tiles_per_seq)
    staging = [
        pltpu.VMEM((d_in // LANES, HALO + MIX_TS, LANES), F32),
        pltpu.VMEM((d_conf // LANES, HALO + MIX_TS, LANES), F32),
        pltpu.VMEM((d_sconv // LANES, HALO + MIX_TS, LANES), F32),
        pltpu.VMEM((MIX_TS, d_mix), BF16),
    ]
    x2 = x.reshape(B * T, D)
    out = pl.pallas_call(
        kern,
        grid=(n_pairs + 1,),
        in_specs=[
            pl.BlockSpec((2 * MIX_TS, D), lambda s: (jnp.minimum(s, n_pairs - 1), 0)),
            pl.BlockSpec((2 * MIX_TS, D), lambda s: (jnp.maximum(s - 1, 0), 0)),
            vec(D),
            _resident((D, d_in), lambda s: (0, 0)),
            _resident((None, d_pool, d_pool), lay),
            vec(d_pool),
            pl.BlockSpec((None, CONF_KERNEL, d_conf), lay),
            vec(d_conf), vec(d_conf), vec(d_conf),
            pl.BlockSpec((None, SCONV_KERNEL, d_sconv), lay),
            _resident((d_mix, D), lambda s: (0, 0)),
            vec(D),
        ],
        out_specs=pl.BlockSpec((2 * MIX_TS, D), lambda s: (jnp.maximum(s - 1, 0), 0)),
        out_shape=jax.ShapeDtypeStruct((B * T, D), F32),
        scratch_shapes=[pltpu.VMEM((MIX_TS, D), BF16), pltpu.VMEM((MIX_TS, D), F32)] + staging + staging,
        compiler_params=pltpu.CompilerParams(
            dimension_semantics=("arbitrary",), vmem_limit_bytes=VMEM_LIMIT),
        name="mixer",
    )(x2, x2, g1, w_in, pmap, psc, cw, cb, lg, lb, sw, w_out, g2)
    return out.reshape(B, T, D)


XATTN_TS = 1024
FF_GROUP = 4


def _xattn_kernel(x_ref, g1_ref, wq_ref, k_ref, v_ref, wo_ref, g2_ref, wup_ref, wdn_ref,
                  o_ref, wup_o_ref, wdn_o_ref, o_scr, *, head_dim):
    wup_o_ref[...] = wup_ref[...].astype(BF16)
    wdn_o_ref[...] = wdn_ref[...].astype(BF16)
    x = x_ref[...]
    h = _rms_scale(x, g1_ref[...]).astype(BF16)
    q = (jnp.dot(h, wq_ref[...], preferred_element_type=F32) * (head_dim ** -0.5)).astype(BF16)
    for hd in range(XATTN_HEADS):
        sl = slice(hd * head_dim, (hd + 1) * head_dim)
        s = lax.dot_general(q[:, sl], k_ref[:, sl], (((1,), (1,)), ((), ())),
                            preferred_element_type=F32)
        m = jnp.max(s, axis=-1, keepdims=True)
        p = jnp.exp(s - m)
        inv_l = 1.0 / jnp.sum(p, axis=-1, keepdims=True)
        o = jnp.dot(p.astype(BF16), v_ref[:, sl], preferred_element_type=F32) * inv_l
        o_scr[:, sl] = o.astype(BF16)
    y = jnp.dot(o_scr[...], wo_ref[...], preferred_element_type=F32)
    o_ref[...] = x + _rms_scale(y, g2_ref[...])


def _xattn_call(l, x, g1, wq, k_all, v_all, wo, g2, n_mem, w_up, w_dn):
    B, T, D = x.shape
    steps = T // XATTN_TS
    up_rows, dn_rows = w_up.shape[1] // (B * steps), w_dn.shape[1] // (B * steps)
    assert up_rows * B * steps == w_up.shape[1] and dn_rows * B * steps == w_dn.shape[1]
    slab = lambda b, j: (l, b * steps + j, 0)
    slab_out = lambda b, j: (b * steps + j, 0)
    lay = lambda b, j: (l, 0, 0)
    vec = pl.BlockSpec((None, 1, D), lay)
    kern = functools.partial(_xattn_kernel, head_dim=D // XATTN_HEADS)
    return pl.pallas_call(
        kern,
        grid=(B, T // XATTN_TS),
        in_specs=[
            pl.BlockSpec((None, XATTN_TS, D), lambda b, j: (b, j, 0)),
            vec,
            _resident((D, D), lambda b, j: (0, 0)),
            pl.BlockSpec((None, n_mem, D), lambda b, j: (l, b, 0)),
            pl.BlockSpec((None, n_mem, D), lambda b, j: (l, b, 0)),
            _resident((D, D), lambda b, j: (0, 0)),
            vec,
            pl.BlockSpec((None, up_rows, w_up.shape[2]), slab),
            pl.BlockSpec((None, dn_rows, w_dn.shape[2]), slab),
        ],
        out_specs=[
            pl.BlockSpec((None, XATTN_TS, D), lambda b, j: (b, j, 0)),
            pl.BlockSpec((up_rows, w_up.shape[2]), slab_out),
            pl.BlockSpec((dn_rows, w_dn.shape[2]), slab_out),
        ],
        out_shape=[
            jax.ShapeDtypeStruct((B, T, D), F32),
            jax.ShapeDtypeStruct(w_up.shape[1:], BF16),
            jax.ShapeDtypeStruct(w_dn.shape[1:], BF16),
        ],
        scratch_shapes=[pltpu.VMEM((XATTN_TS, D), BF16)],
        compiler_params=pltpu.CompilerParams(
            dimension_semantics=("arbitrary", "arbitrary"), vmem_limit_bytes=VMEM_LIMIT),
        name="xattn",
    )(x, g1, wq, k_all, v_all, wo, g2, w_up, w_dn)


def _ffn_kernel(x_ref, g1_ref, wup_ref, cw_ref, wdn_ref, g2_ref, *rest, n_chunks, n_cast):
    cast_in, o_ref, cast_out = rest[:n_cast], rest[n_cast], rest[n_cast + 1:2 * n_cast + 1]
    ha_scr, hb_scr, ua_scr, ub_scr, hal_scr, a_scr = rest[2 * n_cast + 1:]
    for src_ref, dst_ref in zip(cast_in, cast_out):
        dst_ref[...] = src_ref[...].astype(BF16)
    j = pl.program_id(1)
    tiles = FF_CHUNK // LANES

    @pl.when(j == 0)
    def _zero_halo():
        hal_scr[...] = jnp.zeros(hal_scr.shape, F32)

    def up(c, u_scr, h_scr):
        for half in range(2):
            wi = c + half * n_chunks
            col = wi * FF_CHUNK
            if not isinstance(col, int):
                col = pl.multiple_of(col, FF_CHUNK)
            u = jnp.dot(h_scr[...], wup_ref[:, pl.ds(col, FF_CHUNK)],
                        preferred_element_type=F32)
            for i in range(tiles):
                ct = wi * tiles + i
                s = half * tiles + i
                u_scr[s, 0:FFN_HALO, :] = hal_scr[ct]
                u_scr[s, FFN_HALO:FFN_HALO + TS, :] = u[:, i * LANES:(i + 1) * LANES]
                hal_scr[ct] = u[TS - FFN_HALO:TS, i * LANES:(i + 1) * LANES]

    def act(c, u_scr):
        for i in range(tiles):
            taps = []
            for ct in (c * tiles + i, (c + n_chunks) * tiles + i):
                w = cw_ref[ct]
                taps.append([jnp.broadcast_to(w[FFN_CONV_KERNEL - 1 - k:FFN_CONV_KERNEL - k, :], (RC, LANES))
                             for k in range(FFN_CONV_KERNEL)])
            col = c * FF_CHUNK + i * LANES
            if not isinstance(col, int):
                col = pl.multiple_of(col, LANES)
            for r0 in range(0, TS, RC):
                gv = []
                for s, w in ((i, taps[0]), (tiles + i, taps[1])):
                    base = r0 + FFN_HALO
                    acc = w[0] * u_scr[s, base:base + RC, :]
                    for k in range(1, FFN_CONV_KERNEL):
                        acc = acc + w[k] * u_scr[s, base - k:base - k + RC, :]
                    gv.append(acc)
                a = gv[0] * _sigmoid(gv[0]) * gv[1]
                a_scr[r0:r0 + RC, pl.ds(col, LANES)] = a.astype(BF16)

    assert FF_GROUP % 2 == 0
    bufs = (ua_scr, ub_scr)
    n_groups = (n_chunks - 1) // FF_GROUP

    for rows, h_scr in ((slice(0, TS), ha_scr), (slice(TS, 2 * TS), hb_scr)):
        h_scr[...] = _rms_scale(x_ref[rows, :], g1_ref[...]).astype(BF16)

        up(0, ua_scr, h_scr)

        def group(p, carry, h_scr=h_scr):
            c0 = FF_GROUP * p
            for q in range(FF_GROUP):
                up(c0 + q + 1, bufs[(q + 1) % 2], h_scr)
                act(c0 + q, bufs[q % 2])
            return carry

        lax.fori_loop(0, n_groups, group, 0)
        for c in range(FF_GROUP * n_groups, n_chunks):
            if c + 1 < n_chunks:
                up(c + 1, bufs[(c + 1) % 2], h_scr)
            act(c, bufs[c % 2])

        y = jnp.dot(a_scr[...], wdn_ref[...], preferred_element_type=F32)
        o_ref[rows, :] = x_ref[rows, :] + _rms_scale(y, g2_ref[...])


def _ffn_call(l, x, g1, w_up, cw, w_dn, g2, next_weights=()):
    B, T, D = x.shape
    steps = T // (2 * TS)
    slab_rows = [w.shape[1] // (B * steps) for w in next_weights]
    assert all(r * B * steps == w.shape[1] and r % 16 == 0 for r, w in zip(slab_rows, next_weights))
    n_chunks = w_dn.shape[0] // FF_CHUNK
    n_ct = cw.shape[1]
    lay3 = lambda b, j: (l, 0, 0)
    lay4 = lambda b, j: (l, 0, 0, 0)
    vec = pl.BlockSpec((None, 1, D), lay3)
    kern = functools.partial(_ffn_kernel, n_chunks=n_chunks, n_cast=len(next_weights))
    return pl.pallas_call(
        kern,
        grid=(B, T // (2 * TS)),
        in_specs=[
            pl.BlockSpec((None, 2 * TS, D), lambda b, j: (b, j, 0)),
            vec,
            _resident((D, 2 * n_chunks * FF_CHUNK), lambda b, j: (0, 0)),
            pl.BlockSpec((None, n_ct, FFN_CONV_KERNEL, LANES), lay4),
            _resident((n_chunks * FF_CHUNK, D), lambda b, j: (0, 0)),
            vec,
        ] + [pl.BlockSpec((None, r, w.shape[2]), lambda b, j: (l + 1, b * steps + j, 0))
             for r, w in zip(slab_rows, next_weights)],
        out_specs=[pl.BlockSpec((None, 2 * TS, D), lambda b, j: (b, j, 0))]
        + [pl.BlockSpec((r, w.shape[2]), lambda b, j: (b * steps + j, 0)) for r, w in zip(slab_rows, next_weights)],
        out_shape=[jax.ShapeDtypeStruct((B, T, D), F32)]
        + [jax.ShapeDtypeStruct(w.shape[1:], BF16) for w in next_weights],
        scratch_shapes=[
            pltpu.VMEM((TS, D), BF16),
            pltpu.VMEM((TS, D), BF16),
            pltpu.VMEM((2 * FF_CHUNK // LANES, FFN_HALO + TS, LANES), F32),
            pltpu.VMEM((2 * FF_CHUNK // LANES, FFN_HALO + TS, LANES), F32),
            pltpu.VMEM((n_ct, FFN_HALO, LANES), F32),
            pltpu.VMEM((TS, n_chunks * FF_CHUNK), BF16),
        ],
        compiler_params=pltpu.CompilerParams(
            dimension_semantics=("arbitrary", "arbitrary"), vmem_limit_bytes=VMEM_LIMIT),
        name="ffn",
    )(x, g1, w_up, cw, w_dn, g2, *next_weights)


def _block_diag(maps):
    L, G, P, _ = maps.shape
    eye = jnp.eye(G, dtype=maps.dtype)
    return jnp.einsum("lgpq,gh->lgphq", maps, eye).reshape(L, G * P, G * P)


def kernel(x, mem, mem_norm, mix_pre_norm, mix_post_norm, w_in, pool_maps, pool_scale, conf_dw_w, conf_dw_b, conf_ln_g, conf_ln_b, sconv_w, w_out, xattn_pre_norm, xattn_post_norm, xattn_wq, xattn_wk, xattn_wv, xattn_wo, ffn_pre_norm, ffn_post_norm, ffn_w_up, ffn_conv_w, ffn_w_down):
    B, T, D = x.shape
    L = w_in.shape[0]
    n_mem = mem.shape[1]
    d_ff = ffn_w_down.shape[1]
    assert T % (2 * TS) == 0 and T % XATTN_TS == 0 and d_ff % FF_CHUNK == 0 and D % LANES == 0

    row = lambda p: p.reshape(L, 1, p.shape[-1])
    bf = lambda w: w.astype(BF16)

    ffn_cw = ffn_conv_w.reshape(L, FFN_CONV_KERNEL, 2 * d_ff // LANES, LANES).transpose(0, 2, 1, 3)
    pmap = bf(_block_diag(pool_maps))

    per_layer = (w_in, w_out, xattn_wq, xattn_wo)
    k_all, v_all, w_in_b, w_out_b, wq_b, wo_b = _kv_call(
        mem.reshape(B * n_mem, D), mem_norm.reshape(1, D), xattn_wk, xattn_wv, per_layer)

    for l in range(L):
        x = _mixer_call(l, x, row(mix_pre_norm), w_in_b, pmap, row(pool_scale), conf_dw_w,
                        row(conf_dw_b), row(conf_ln_g), row(conf_ln_b), sconv_w, w_out_b,
                        row(mix_post_norm))
        x, w_up_b, w_dn_b = _xattn_call(l, x, row(xattn_pre_norm), wq_b, k_all, v_all, wo_b,
                                        row(xattn_post_norm), n_mem, ffn_w_up, ffn_w_down)
        nxt = per_layer if l + 1 < L else ()
        x, *cast = _ffn_call(l, x, row(ffn_pre_norm), w_up_b, ffn_cw, w_dn_b, row(ffn_post_norm), nxt)
        if cast:
            w_in_b, w_out_b, wq_b, wo_b = cast
    return x
```
